```python
import jax
import jax.numpy as jnp
from jax import lax
import numpy as np

D_MODEL = 1024
BATCH = 8
SEQ = 4096
DEPTH = 2

N_MIXERS = 2
EPS = 1e-6

A_HEADS = 16
A_GROUPS = 4
A_HEAD_DIM = 64
A_WIDTH = A_HEADS * A_HEAD_DIM
CMP_LEN = 32
CMP_STRIDE = 16
CMP_HIDDEN = 256
SLC_LEN = 64
SLC_TOP = 16
WIN = 512
Q_BLOCK = 64
A_KV_COLS = 6 * A_GROUPS * A_HEAD_DIM
A_SPLITS = (A_WIDTH, A_WIDTH + A_KV_COLS, A_WIDTH + A_KV_COLS + 3 * A_HEADS)
A_IN_COLS = 2 * A_WIDTH + A_KV_COLS + 3 * A_HEADS

B_HEADS = 8
B_QK_DIM = 128
B_V_DIM = 256
B_WIDTH = B_HEADS * B_V_DIM
B_QK_COLS = 2 * B_HEADS * B_QK_DIM
CONV_WIDTH = 4
CHUNK = 64
B_SPLITS = (B_QK_COLS, B_QK_COLS + B_WIDTH, B_QK_COLS + B_WIDTH + B_HEADS,
            B_QK_COLS + B_WIDTH + 2 * B_HEADS, B_QK_COLS + 2 * B_WIDTH + 2 * B_HEADS)
B_IN_COLS = B_QK_COLS + 3 * B_WIDTH + 2 * B_HEADS

kernel_name = 'nsa_mlstm_hybrid_block'


def rmsnorm(x, g):
    xf = x.astype(jnp.float32)
    xf = xf * lax.rsqrt(jnp.mean(xf * xf, axis=-1, keepdims=True) + EPS)
    return xf.astype(x.dtype) * g


def alibi_slopes(n):
    return jnp.asarray(2.0 ** (-8.0 * np.arange(1, n + 1) / n), jnp.float32)


def masked_probs(s, mask):
    s = jnp.where(mask, s, -jnp.inf)
    m = jnp.max(s, axis=-1, keepdims=True)
    m = jnp.where(jnp.isfinite(m), m, 0.0)
    p = jnp.exp(s - m)
    return p / jnp.maximum(jnp.sum(p, axis=-1, keepdims=True), 1e-30)


def selection_overlap(seq):
    n_cmp = seq // CMP_STRIDE - CMP_LEN // CMP_STRIDE + 1
    n_sel = seq // SLC_LEN
    c0 = np.arange(n_cmp)[:, None] * CMP_STRIDE
    s0 = np.arange(n_sel)[None, :] * SLC_LEN
    ov = np.clip(np.minimum(c0 + CMP_LEN, s0 + SLC_LEN) - np.maximum(c0, s0), 0, None)
    return jnp.asarray(ov / CMP_LEN, jnp.float32)


def compress_tokens(kv, pe, w1, b1, w2, b2):
    B, S, G, DH = kv.shape
    r = CMP_LEN // CMP_STRIDE
    n_chunk = S // CMP_STRIDE
    n_cmp = n_chunk - r + 1
    chunks = kv.reshape(B, n_chunk, CMP_STRIDE, G, DH)
    blocks = jnp.concatenate([chunks[:, i:i + n_cmp] for i in range(r)], axis=2) + pe[:, None, :]
    flat = blocks.transpose(0, 1, 3, 2, 4).reshape(B, n_cmp, G, CMP_LEN * DH)
    return jax.nn.gelu(flat @ w1 + b1) @ w2 + b2


def nsa_mixer(h, w_in, cmp_pe, cmp_w1, cmp_b1, cmp_w2, cmp_b2, w_out):
    B, S, _ = h.shape
    G, HG, DH = A_GROUPS, A_HEADS // A_GROUPS, A_HEAD_DIM
    f32 = jnp.float32
    scale = DH ** -0.5
    slopes = alibi_slopes(A_HEADS).reshape(G, HG)
    q, kv, gate_logits, z = jnp.split(h @ w_in, A_SPLITS, axis=-1)
    q = q.reshape(B, S, G, HG, DH)
    kv = kv.reshape(B, S, 6, G, DH)
    t = jnp.arange(S)

    k_cmp = compress_tokens(kv[:, :, 0], cmp_pe[0], cmp_w1[0], cmp_b1[0], cmp_w2[0], cmp_b2[0])
    v_cmp = compress_tokens(kv[:, :, 1], cmp_pe[1], cmp_w1[1], cmp_b1[1], cmp_w2[1], cmp_b2[1])
    n_cmp = k_cmp.shape[1]
    end_c = jnp.arange(n_cmp) * CMP_STRIDE + (CMP_LEN - 1)
    dist_c = (t[:, None] - end_c[None, :]).astype(f32)
    s_cmp = jnp.einsum('bsghd,bcgd->bghsc', q, k_cmp).astype(f32) * scale - slopes[:, :, None, None] * dist_c
    p_cmp = masked_probs(s_cmp, dist_c >= 0)
    o_cmp = jnp.einsum('bghsc,bcgd->bsghd', p_cmp.astype(v_cmp.dtype), v_cmp)

    n_sel = S // SLC_LEN
    n_top = min(SLC_TOP, n_sel)
    imp = jnp.einsum('bghsc,cj->bgsj', p_cmp, selection_overlap(S))
    blk = jnp.arange(n_sel)[None, :]
    cur = (t // SLC_LEN)[:, None]
    forced = (blk == 0) | (blk == cur) | (blk == cur - 1)
    score = jnp.where(forced, jnp.inf, jnp.where(blk <= cur, imp, -jnp.inf))
    _, sel_idx = lax.top_k(score, n_top)
    k_slc = kv[:, :, 2].reshape(B, n_sel, SLC_LEN, G, DH).transpose(0, 3, 1, 2, 4)
    v_slc = kv[:, :, 3].reshape(B, n_sel, SLC_LEN, G, DH).transpose(0, 3, 1, 2, 4)
    gather = jax.vmap(jax.vmap(lambda blocks, ix: blocks[ix]))

    pad = ((0, 0), (WIN, 0), (0, 0), (0, 0))
    k_win = jnp.pad(kv[:, :, 4], pad)
    v_win = jnp.pad(kv[:, :, 5], pad)

    n_qb = S // Q_BLOCK
    q_blocks = jnp.moveaxis(q.reshape(B, n_qb, Q_BLOCK, G, HG, DH), 1, 0)
    idx_blocks = jnp.moveaxis(sel_idx.reshape(B, G, n_qb, Q_BLOCK, n_top), 2, 0)

    def query_block(args):
        j, qb, ib = args
        q0 = j * Q_BLOCK
        tq = q0 + jnp.arange(Q_BLOCK)
        ks = gather(k_slc, ib)
        vs = gather(v_slc, ib)
        key_pos = ib[..., None] * SLC_LEN + jnp.arange(SLC_LEN)
        dist = (tq[None, None, :, None, None] - key_pos).astype(f32)[:, :, None]
        s = jnp.einsum('bqghd,bgqnkd->bghqnk', qb, ks).astype(f32) * scale - slopes[None, :, :, None, None, None] * dist
        p = masked_probs(s.reshape(B, G, HG, Q_BLOCK, -1), (dist >= 0).reshape(B, G, 1, Q_BLOCK, -1))
        o_s = jnp.einsum('bghqm,bgqmd->bqghd', p.astype(vs.dtype), vs.reshape(B, G, Q_BLOCK, -1, DH))
        kw = lax.dynamic_slice_in_dim(k_win, q0, WIN + Q_BLOCK, axis=1)
        vw = lax.dynamic_slice_in_dim(v_win, q0, WIN + Q_BLOCK, axis=1)
        kpos = q0 - WIN + jnp.arange(WIN + Q_BLOCK)
        dist_w = tq[:, None] - kpos[None, :]
        mask_w = (dist_w >= 0) & (dist_w < WIN) & (kpos[None, :] >= 0)
        s = jnp.einsum('bqghd,bkgd->bghqk', qb, kw).astype(f32) * scale - slopes[None, :, :, None, None] * dist_w.astype(f32)
        p = masked_probs(s, mask_w)
        o_w = jnp.einsum('bghqk,bkgd->bqghd', p.astype(vw.dtype), vw)
        return o_s, o_w

    o_slc, o_win = lax.map(query_block, (jnp.arange(n_qb), q_blocks, idx_blocks))
    o_slc = jnp.moveaxis(o_slc, 0, 1).reshape(B, S, G, HG, DH)
    o_win = jnp.moveaxis(o_win, 0, 1).reshape(B, S, G, HG, DH)

    gates = jax.nn.sigmoid(gate_logits).reshape(B, S, 3, G, HG, 1)
    o = gates[:, :, 0] * o_cmp + gates[:, :, 1] * o_slc + gates[:, :, 2] * o_win
    y = o.reshape(B, S, A_WIDTH) * jax.nn.silu(z)
    return y @ w_out


def causal_conv(x, w, b):
    C = x.shape[-1]
    y = lax.conv_general_dilated(x, w[:, None, :], window_strides=(1,), padding=[(CONV_WIDTH - 1, 0)],
                                 dimension_numbers=('NWC', 'WIO', 'NWC'), feature_group_count=C)
    return y + b


def mlstm_chunkwise(q, k, v, i_pre, log_f):
    B, H, S, DK = q.shape
    DV = v.shape[-1]
    N, L = S // CHUNK, CHUNK
    f32 = jnp.float32
    q = q.reshape(B, H, N, L, DK)
    k = k.reshape(B, H, N, L, DK)
    v = v.reshape(B, H, N, L, DV)
    i_pre = i_pre.reshape(B, H, N, L)
    b = jnp.cumsum(log_f.reshape(B, H, N, L), axis=-1)
    g = b[..., -1]
    a = g[..., None] - b + i_pre
    m_loc = jnp.max(a, axis=-1)
    w = jnp.exp(a - m_loc[..., None])
    c_loc = jnp.einsum('bhnlk,bhnlv->bhnkv', k * w[..., None], v)
    n_loc = jnp.einsum('bhnl,bhnlk->bhnk', w, k)

    def step(carry, xs):
        c_st, n_st, m_st = carry
        c_l, n_l, m_l, g_l = xs
        m_new = jnp.maximum(g_l + m_st, m_l)
        s_old = jnp.exp(g_l + m_st - m_new)
        s_new = jnp.exp(m_l - m_new)
        c_next = s_old[..., None, None] * c_st + s_new[..., None, None] * c_l
        n_next = s_old[..., None] * n_st + s_new[..., None] * n_l
        return (c_next, n_next, m_new), (c_st, n_st, m_st)

    init = (jnp.zeros((B, H, DK, DV), f32), jnp.zeros((B, H, DK), f32), jnp.zeros((B, H), f32))
    xs = (jnp.moveaxis(c_loc, 2, 0), jnp.moveaxis(n_loc, 2, 0), jnp.moveaxis(m_loc, 2, 0), jnp.moveaxis(g, 2, 0))
    _, (c_prev, n_prev, m_prev) = lax.scan(step, init, xs)
    c_prev = jnp.moveaxis(c_prev, 0, 2)
    n_prev = jnp.moveaxis(n_prev, 0, 2)
    m_prev = jnp.moveaxis(m_prev, 0, 2)

    causal = jnp.tril(jnp.ones((L, L), bool))
    log_d = jnp.where(causal, b[..., :, None] - b[..., None, :] + i_pre[..., None, :], -jnp.inf)
    m_inter = b + m_prev[..., None]
    m_j = jnp.maximum(jnp.max(log_d, axis=-1), m_inter)
    s = jnp.einsum('bhnjk,bhnsk->bhnjs', q, k) * jnp.exp(log_d - m_j[..., None])
    w_inter = jnp.exp(m_inter - m_j)
    num = jnp.einsum('bhnjs,bhnsv->bhnjv', s, v) + w_inter[..., None] * jnp.einsum('bhnjk,bhnkv->bhnjv', q, c_prev)
    den = jnp.sum(s, axis=-1) + w_inter * jnp.einsum('bhnjk,bhnk->bhnj', q, n_prev)
    h = num / jnp.maximum(jnp.abs(den), jnp.exp(-m_j))[..., None]
    return h.reshape(B, H, S, DV)


def mlstm_mixer(h, w_in, conv_w, conv_b, gate_b, head_g, w_out):
    B, S, _ = h.shape
    H, DK, DV = B_HEADS, B_QK_DIM, B_V_DIM
    f32 = jnp.float32
    qk, v, gi, gf, og, z = jnp.split(h @ w_in, B_SPLITS, axis=-1)
    qk = jax.nn.silu(causal_conv(qk, conv_w, conv_b))
    q, k = jnp.split(qk, 2, axis=-1)
    q = q.reshape(B, S, H, DK).transpose(0, 2, 1, 3).astype(f32)
    k = k.reshape(B, S, H, DK).transpose(0, 2, 1, 3).astype(f32) * (DK ** -0.5)
    v = v.reshape(B, S, H, DV).transpose(0, 2, 1, 3).astype(f32)
    i_pre = (gi + gate_b[0]).astype(f32).transpose(0, 2, 1)
    log_f = jax.nn.log_sigmoid((gf + gate_b[1]).astype(f32)).transpose(0, 2, 1)
    hh = mlstm_chunkwise(q, k, v, i_pre, log_f)
    hh = hh * lax.rsqrt(jnp.mean(hh * hh, axis=-1, keepdims=True) + EPS)
    hh = hh.transpose(0, 2, 1, 3).reshape(B, S, B_WIDTH).astype(h.dtype) * head_g
    y = jax.nn.sigmoid(og) * hh * jax.nn.silu(z)
    return y @ w_out


def setup_inputs(seed: int = 0) -> dict:
    key = jax.random.key(seed)
    ks = jax.random.split(key, 20)
    D = D_MODEL
    n_a = (DEPTH + 1) // 2
    n_b = DEPTH // 2

    def nrm(k, shape, s):
        return jax.random.normal(k, shape, jnp.float32) * s

    x = nrm(ks[0], (BATCH, SEQ, D), 1.0)
    c = nrm(ks[1], (BATCH, D), 1.0)
    ada_w = nrm(ks[2], (DEPTH, D, 3 * D), 0.3 * D ** -0.5)
    ada_b = nrm(ks[3], (DEPTH, 3 * D), 0.01)
    norm_g = 1.0 + nrm(ks[4], (DEPTH, D), 0.02)
    final_g = 1.0 + nrm(ks[5], (D,), 0.02)
    a_w_in = nrm(ks[6], (n_a, D, A_IN_COLS), D ** -0.5)
    a_cmp_pe = nrm(ks[7], (n_a, 2, CMP_LEN, A_HEAD_DIM), 0.1)
    a_cmp_w1 = nrm(ks[8], (n_a, 2, CMP_LEN * A_HEAD_DIM, CMP_HIDDEN), (CMP_LEN * A_HEAD_DIM) ** -0.5)
    a_cmp_b1 = nrm(ks[9], (n_a, 2, CMP_HIDDEN), 0.01)
    a_cmp_w2 = nrm(ks[10], (n_a, 2, CMP_HIDDEN, A_HEAD_DIM), CMP_HIDDEN ** -0.5)
    a_cmp_b2 = nrm(ks[11], (n_a, 2, A_HEAD_DIM), 0.01)
    a_w_out = nrm(ks[12], (n_a, A_WIDTH, D), A_WIDTH ** -0.5)
    b_w_in = nrm(ks[13], (n_b, D, B_IN_COLS), D ** -0.5)
    b_conv_w = nrm(ks[14], (n_b, CONV_WIDTH, B_QK_COLS), CONV_WIDTH ** -0.5)
    b_conv_b = nrm(ks[15], (n_b, B_QK_COLS), 0.01)
    b_i_bias = nrm(ks[16], (n_b, B_HEADS), 0.1)
    b_f_bias = jnp.linspace(3.0, 6.0, B_HEADS, dtype=jnp.float32)[None, :] + nrm(ks[17], (n_b, B_HEADS), 0.1)
    b_gate_b = jnp.stack([b_i_bias, b_f_bias], axis=1)
    b_head_g = 1.0 + nrm(ks[18], (n_b, B_WIDTH), 0.02)
    b_w_out = nrm(ks[19], (n_b, B_WIDTH, D), B_WIDTH ** -0.5)
    return {'x': x, 'c': c, 'ada_w': ada_w, 'ada_b': ada_b, 'norm_g': norm_g, 'final_g': final_g,
            'a_w_in': a_w_in, 'a_cmp_pe': a_cmp_pe, 'a_cmp_w1': a_cmp_w1, 'a_cmp_b1': a_cmp_b1,
            'a_cmp_w2': a_cmp_w2, 'a_cmp_b2': a_cmp_b2, 'a_w_out': a_w_out,
            'b_w_in': b_w_in, 'b_conv_w': b_conv_w, 'b_conv_b': b_conv_b, 'b_gate_b': b_gate_b,
            'b_head_g': b_head_g, 'b_w_out': b_w_out}


def reference(x, c, ada_w, ada_b, norm_g, final_g,
              a_w_in, a_cmp_pe, a_cmp_w1, a_cmp_b1, a_cmp_w2, a_cmp_b2, a_w_out,
              b_w_in, b_conv_w, b_conv_b, b_gate_b, b_head_g, b_w_out):
    for i in range(DEPTH):
        shift, scale, gate = jnp.split(c @ ada_w[i] + ada_b[i], 3, axis=-1)
        h = rmsnorm(x, norm_g[i]) * (1.0 + scale[:, None, :]) + shift[:, None, :]
        li = i // N_MIXERS
        if i % N_MIXERS == 0:
            y = nsa_mixer(h, a_w_in[li], a_cmp_pe[li], a_cmp_w1[li], a_cmp_b1[li],
                          a_cmp_w2[li], a_cmp_b2[li], a_w_out[li])
        else:
            y = mlstm_mixer(h, b_w_in[li], b_conv_w[li], b_conv_b[li], b_gate_b[li],
                            b_head_g[li], b_w_out[li])
        x = x + gate[:, None, :] * y
    return rmsnorm(x, final_g)
```

```python
import functools

import numpy as np
import jax
import jax.numpy as jnp
from jax import lax
from jax.experimental import pallas as pl
from jax.experimental.pallas import tpu as pltpu

F32 = jnp.float32
BF16 = jnp.bfloat16
HIGHEST = lax.Precision.HIGHEST

EPS = 1e-6
NEG = -1e30

D_MODEL = 1024
A_HEADS = 16
A_GROUPS = 4
A_HG = A_HEADS // A_GROUPS
A_DH = 64
A_WIDTH = A_HEADS * A_DH
CMP_LEN = 32
CMP_STRIDE = 16
CMP_HIDDEN = 256
SLC_LEN = 64
SLC_TOP = 16
WIN = 512
B_HEADS = 8
B_DK = 128
B_DV = 256
B_WIDTH = B_HEADS * B_DV
B_QK = 2 * B_HEADS * B_DK
CONV_W = 4

LANES = 128
SUBLANES = 8
VMEM_LIMIT = 56 * 1024 * 1024

TQ = 256
ML_CHUNK = 256
TM_PROJ = 1024
TN_PROJ = 512
TM_OUT = 512
TM_CONV = 1024
TC_CONV = 512


def _dot(a, b, **kw):
    return jnp.dot(a, b, preferred_element_type=F32, **kw)


def _dot_nt(a, b):
    return lax.dot_general(a, b, (((1,), (1,)), ((), ())), preferred_element_type=F32)


def _cparams(sem):
    return pltpu.CompilerParams(dimension_semantics=sem, vmem_limit_bytes=VMEM_LIMIT)


def _silu(x):
    return x * jax.nn.sigmoid(x)


def _ada_kernel(c_ref, w_ref, b_ref, o_ref):
    o_ref[...] = _dot(c_ref[...], w_ref[...], precision=HIGHEST) + b_ref[...]


def _ada_mod(c, ada_w, ada_b):
    depth, d, d3 = ada_w.shape
    bsz = c.shape[0]
    nj = d3 // d
    return pl.pallas_call(
        _ada_kernel,
        grid=(depth, nj),
        in_specs=[
            pl.BlockSpec((bsz, d), lambda i, j: (0, 0)),
            pl.BlockSpec((None, d, d), lambda i, j: (i, 0, j)),
            pl.BlockSpec((None, 1, d), lambda i, j: (i, 0, j)),
        ],
        out_specs=pl.BlockSpec((None, bsz, d), lambda i, j: (i, 0, j)),
        out_shape=jax.ShapeDtypeStruct((depth, bsz, d3), F32),
        compiler_params=_cparams(("parallel", "parallel")),
        name="ada_mod",
    )(c, ada_w, ada_b.reshape(depth, 1, d3))


def _normmod_kernel(x_ref, g_ref, sc_ref, sh_ref, w_ref, wg_ref, o_ref, og_ref, h_ref):
    @pl.when(pl.program_id(1) == 0)
    def _():
        x = x_ref[...]
        xn = x * lax.rsqrt(jnp.mean(x * x, axis=-1, keepdims=True) + EPS)
        h = (xn * g_ref[...]) * (1.0 + sc_ref[...]) + sh_ref[...]
        hb = h.astype(BF16)
        h_ref[...] = hb
        og_ref[...] = _dot(hb, wg_ref[...])

    o_ref[...] = _dot(h_ref[...], w_ref[...]).astype(o_ref.dtype)


def _normmod_matmul(x2d, g, scale, shift, w, wg, seq, name):
    m, d = x2d.shape
    n = w.shape[1]
    ng = wg.shape[1]
    tm, tn = TM_PROJ, TN_PROJ
    assert m % tm == 0 and n % tn == 0 and seq % tm == 0
    per_b = seq // tm
    return pl.pallas_call(
        _normmod_kernel,
        grid=(m // tm, n // tn),
        in_specs=[
            pl.BlockSpec((tm, d), lambda i, j: (i, 0)),
            pl.BlockSpec((1, d), lambda i, j: (0, 0)),
            pl.BlockSpec((None, 1, d), lambda i, j: (i // per_b, 0, 0)),
            pl.BlockSpec((None, 1, d), lambda i, j: (i // per_b, 0, 0)),
            pl.BlockSpec((d, tn), lambda i, j: (0, j)),
            pl.BlockSpec((d, ng), lambda i, j: (0, 0)),
        ],
        out_specs=[
            pl.BlockSpec((tm, tn), lambda i, j: (i, j)),
            pl.BlockSpec((tm, ng), lambda i, j: (i, 0)),
        ],
        out_shape=[jax.ShapeDtypeStruct((m, n), BF16), jax.ShapeDtypeStruct((m, ng), F32)],
        scratch_shapes=[pltpu.VMEM((tm, d), BF16)],
        compiler_params=_cparams(("parallel", "arbitrary")),
        name=name,
    )(x2d, g.reshape(1, d), scale[:, None, :], shift[:, None, :], w, wg)


def _compress_kernel(x_ref, wc_ref, pe_ref, b1_ref, w2_ref, b2_ref, o_ref):
    a = _dot(x_ref[...], wc_ref[...])
    pb = _dot(pe_ref[...], wc_ref[...])
    n = a.shape[0]
    hid = CMP_HIDDEN
    hs = []
    for kv in range(2):
        c0 = kv * 2 * hid
        first = a[:, c0:c0 + hid]
        second = pltpu.roll(a[:, c0 + hid:c0 + 2 * hid], n - 1, 0)
        bias = pb[0:1, c0:c0 + hid] + pb[1:2, c0 + hid:c0 + 2 * hid] + b1_ref[:, kv * hid:(kv + 1) * hid]
        hs.append(jax.nn.gelu(first + second + bias))
    h = jnp.concatenate(hs, axis=1).astype(BF16)
    o_ref[...] = (_dot(h, w2_ref[...]) + b2_ref[...]).astype(o_ref.dtype)


def _compress(xc, wc, pec, b1c, w2a, b2a):
    bsz, g, nchunk, kdim = xc.shape
    return pl.pallas_call(
        _compress_kernel,
        grid=(bsz, g),
        in_specs=[
            pl.BlockSpec((None, None, nchunk, kdim), lambda b, gi: (b, gi, 0, 0)),
            pl.BlockSpec(wc.shape, lambda b, gi: (0, 0)),
            pl.BlockSpec(pec.shape, lambda b, gi: (0, 0)),
            pl.BlockSpec(b1c.shape, lambda b, gi: (0, 0)),
            pl.BlockSpec(w2a.shape, lambda b, gi: (0, 0)),
            pl.BlockSpec(b2a.shape, lambda b, gi: (0, 0)),
        ],
        out_specs=pl.BlockSpec((None, None, nchunk, 4 * LANES), lambda b, gi: (b, gi, 0, 0)),
        out_shape=jax.ShapeDtypeStruct((bsz, g, nchunk, 4 * LANES), BF16),
        compiler_params=_cparams(("parallel", "parallel")),
        name="compress_tokens",
    )(xc, wc, pec, b1c, w2a, b2a)


def _pair_gate(gl_ref, c0, c1, rows):
    lane = lax.broadcasted_iota(jnp.int32, (rows, LANES), 1)
    g0 = jax.nn.sigmoid(gl_ref[:, c0:c0 + 1])
    g1 = jax.nn.sigmoid(gl_ref[:, c1:c1 + 1])
    return jnp.where(lane < A_DH, g0, g1)


def _cmp_attn_kernel(q_ref, kv_ref, gl_ref, sl_ref, ov_ref, o_ref, sb_ref, *, n_cmp, n_sel, n_top):
    tq = q_ref.shape[0]
    ncol = kv_ref.shape[0]
    q0 = pl.program_id(2) * tq
    row = lax.broadcasted_iota(jnp.int32, (tq, ncol), 0) + q0
    col = lax.broadcasted_iota(jnp.int32, (tq, ncol), 1)
    dist = row - (col * CMP_STRIDE + (CMP_LEN - 1))
    valid = (dist >= 0) & (col < n_cmp)
    distf = dist.astype(F32)
    psum = jnp.zeros((tq, ncol), F32)
    outs = []
    for p in range(2):
        qp = q_ref[:, p * LANES:(p + 1) * LANES]
        acc = jnp.zeros((tq, LANES), F32)
        for e in range(2):
            hg = 2 * p + e
            slope = sl_ref[hg:hg + 1, 0:1]
            s = _dot_nt(qp, kv_ref[:, e * LANES:(e + 1) * LANES]) - slope * distf
            s = jnp.where(valid, s, NEG)
            m = jnp.max(s, axis=1, keepdims=True)
            m = jnp.where(m > 0.5 * NEG, m, 0.0)
            pr = jnp.exp(s - m)
            pr = pr / jnp.maximum(jnp.sum(pr, axis=1, keepdims=True), 1e-30)
            psum = psum + pr
            acc = acc + _dot(pr.astype(BF16), kv_ref[:, (2 + e) * LANES:(3 + e) * LANES])
        outs.append(acc * _pair_gate(gl_ref, 2 * p, 2 * p + 1, tq))
    o_ref[...] = jnp.concatenate(outs, axis=1).astype(o_ref.dtype)

    imp = _dot(psum, ov_ref[...], precision=HIGHEST)
    lane = lax.broadcasted_iota(jnp.int32, (tq, LANES), 1)
    lanef = lane.astype(F32)
    t = lax.broadcasted_iota(jnp.int32, (tq, LANES), 0) + q0
    cur = lax.shift_right_logical(t, int(np.log2(SLC_LEN)))
    forced = (lane == 0) | (lane == cur) | (lane == cur - 1)
    score = jnp.where(forced, 3e38, jnp.where(lane <= cur, imp, -1.0))
    score = jnp.where(lane < n_sel, score, -2.0)
    sel = jnp.zeros((tq, LANES), F32)
    for _ in range(n_top):
        mx = jnp.max(score, axis=1, keepdims=True)
        idx = jnp.min(jnp.where(score == mx, lanef, 1e9), axis=1, keepdims=True)
        pick = lanef == idx
        sel = jnp.where(pick, 1.0, sel)
        score = jnp.where(pick, -3.0, score)
    sb_ref[...] = jnp.where((sel > 0.0) | (lane >= n_sel), 0.0, NEG).astype(sb_ref.dtype)


def _cmp_attn(a0, kvc, gl, slopes, ov, bsz, seq, n_cmp, n_sel, n_top):
    nq = seq // TQ
    ncol = kvc.shape[2]
    kern = functools.partial(_cmp_attn_kernel, n_cmp=n_cmp, n_sel=n_sel, n_top=n_top)
    return pl.pallas_call(
        kern,
        grid=(bsz, A_GROUPS, nq),
        in_specs=[
            pl.BlockSpec((TQ, 2 * LANES), lambda b, g, i: (b * nq + i, g)),
            pl.BlockSpec((None, None, ncol, 4 * LANES), lambda b, g, i: (b, g, 0, 0)),
            pl.BlockSpec((TQ, LANES), lambda b, g, i: (b * nq + i, g)),
            pl.BlockSpec((None, SUBLANES, LANES), lambda b, g, i: (g, 0, 0)),
            pl.BlockSpec(ov.shape, lambda b, g, i: (0, 0)),
        ],
        out_specs=[
            pl.BlockSpec((TQ, 2 * LANES), lambda b, g, i: (b * nq + i, g)),
            pl.BlockSpec((None, None, TQ, LANES), lambda b, g, i: (b, g, i, 0)),
        ],
        out_shape=[
            jax.ShapeDtypeStruct((bsz * seq, A_WIDTH), BF16),
            jax.ShapeDtypeStruct((bsz, A_GROUPS, seq, LANES), BF16),
        ],
        compiler_params=_cparams(("parallel", "parallel", "parallel")),
        name="cmp_attention_topk",
    )(a0, kvc, gl, slopes, ov)


def _split_kv(kv_ref):
    kvf = kv_ref[...].astype(F32)
    lane = lax.broadcasted_iota(jnp.int32, kvf.shape, 1)
    rolled = pltpu.roll(kvf, A_DH, 1)
    low = lane < A_DH
    klo = jnp.where(low, kvf, 0.0)
    khi = jnp.where(low, 0.0, rolled)
    vlo = jnp.where(low, rolled, 0.0)
    vhi = jnp.where(low, 0.0, kvf)
    return klo, khi, vlo, vhi


def _slc_kernel(q_ref, sb_ref, kv_ref, gl_ref, sl_ref, o_ref,
                ka_ref, va_ref, sd_ref, bd_ref, m_ref, l_ref, acc_ref):
    tq = q_ref.shape[0]
    tk = tq
    seq = kv_ref.shape[0]
    i = pl.program_id(2)

    @pl.when(i == 0)
    def _stage():
        klo, khi, vlo, vhi = _split_kv(kv_ref)
        lane = lax.broadcasted_iota(jnp.int32, (seq, LANES), 1)
        blk = lax.shift_right_logical(lax.broadcasted_iota(jnp.int32, (seq, LANES), 0), int(np.log2(SLC_LEN)))
        onehot = jnp.where(blk == lane, 1.0, 0.0).astype(BF16)
        ka_ref[0, :, 0:LANES] = klo.astype(BF16)
        ka_ref[1, :, 0:LANES] = khi.astype(BF16)
        ka_ref[0, :, LANES:2 * LANES] = onehot
        ka_ref[1, :, LANES:2 * LANES] = onehot
        va_ref[0] = vlo.astype(BF16)
        va_ref[1] = vhi.astype(BF16)
        r = lax.broadcasted_iota(jnp.int32, (tq, tk), 0)
        c = lax.broadcasted_iota(jnp.int32, (tq, tk), 1)
        d = (r - c).astype(F32)
        for hg in range(A_HG):
            sd = -sl_ref[hg:hg + 1, 0:1] * d
            sd_ref[hg] = sd
            bd_ref[hg] = jnp.where(c <= r, sd, NEG)

    sb = sb_ref[...]
    qa = [jnp.concatenate([q_ref[:, p * LANES:(p + 1) * LANES], sb], axis=1) for p in range(2)]
    m_ref[...] = jnp.full(m_ref.shape, NEG, F32)
    l_ref[...] = jnp.zeros(l_ref.shape, F32)
    acc_ref[...] = jnp.zeros(acc_ref.shape, F32)
    lane = lax.broadcasted_iota(jnp.int32, (tq, LANES), 1)
    low = lane < A_DH

    def tile(kt, bias_ref, off):
        k0 = pl.multiple_of(kt * tk, tk)
        for p in range(2):
            alphas = []
            pv = jnp.zeros((tq, LANES), F32)
            for e in range(2):
                hg = 2 * p + e
                cst = -sl_ref[hg:hg + 1, 0:1] * off
                s = _dot_nt(qa[p], ka_ref[e, pl.ds(k0, tk), :]) + bias_ref[hg]
                m_prev = m_ref[hg]
                m_new = jnp.maximum(m_prev, jnp.max(s, axis=1, keepdims=True) + cst)
                alpha = jnp.exp(m_prev - m_new)
                pm = jnp.exp(s - (m_new - cst))
                l_ref[hg] = alpha * l_ref[hg] + jnp.sum(pm, axis=1, keepdims=True)
                m_ref[hg] = m_new
                alphas.append(alpha)
                pv = pv + _dot(pm.astype(BF16), va_ref[e, pl.ds(k0, tk), :])
            acc_ref[p] = acc_ref[p] * jnp.where(low, alphas[0], alphas[1]) + pv

    def body(kt, carry):
        tile(kt, sd_ref, ((i - kt) * tk).astype(F32))
        return carry

    lax.fori_loop(0, i, body, 0)
    tile(i, bd_ref, jnp.float32(0.0))

    outs = []
    for p in range(2):
        lp = jnp.where(low, l_ref[2 * p], l_ref[2 * p + 1])
        gate = _pair_gate(gl_ref, A_HG + 2 * p, A_HG + 2 * p + 1, tq)
        outs.append(acc_ref[p] / jnp.maximum(lp, 1e-30) * gate)
    o_ref[...] = jnp.concatenate(outs, axis=1).astype(o_ref.dtype)


def _slc_attn(a0, sb, gl, slopes, bsz, seq, kv_col):
    nq = seq // TQ
    return pl.pallas_call(
        _slc_kernel,
        grid=(bsz, A_GROUPS, nq),
        in_specs=[
            pl.BlockSpec((TQ, 2 * LANES), lambda b, g, i: (b * nq + i, g)),
            pl.BlockSpec((None, None, TQ, LANES), lambda b, g, i: (b, g, i, 0)),
            pl.BlockSpec((seq, LANES), lambda b, g, i: (b, kv_col + g)),
            pl.BlockSpec((TQ, LANES), lambda b, g, i: (b * nq + i, g)),
            pl.BlockSpec((None, SUBLANES, LANES), lambda b, g, i: (g, 0, 0)),
        ],
        out_specs=pl.BlockSpec((TQ, 2 * LANES), lambda b, g, i: (b * nq + i, g)),
        out_shape=jax.ShapeDtypeStruct((bsz * seq, A_WIDTH), BF16),
        scratch_shapes=[
            pltpu.VMEM((2, seq, 2 * LANES), BF16),
            pltpu.VMEM((2, seq, LANES), BF16),
            pltpu.VMEM((A_HG, TQ, TQ), F32),
            pltpu.VMEM((A_HG, TQ, TQ), F32),
            pltpu.VMEM((A_HG, TQ, 1), F32),
            pltpu.VMEM((A_HG, TQ, 1), F32),
            pltpu.VMEM((2, TQ, LANES), F32),
        ],
        compiler_params=_cparams(("parallel", "parallel", "arbitrary")),
        name="selected_attention",
    )(a0, sb, a0, gl, slopes)


def _win_kernel(q_ref, kv_ref, gl_ref, sl_ref, o_ref, ka_ref, va_ref, bw_ref):
    tq = q_ref.shape[0]
    seq = kv_ref.shape[0]
    wk = WIN + tq
    i = pl.program_id(2)

    @pl.when(i == 0)
    def _stage():
        klo, khi, vlo, vhi = _split_kv(kv_ref)
        zeros = jnp.zeros((seq, LANES), BF16)
        lane_p = lax.broadcasted_iota(jnp.int32, (WIN, LANES), 1)
        padflag = jnp.where(lane_p == 0, 1.0, 0.0).astype(BF16)
        zpad = jnp.zeros((WIN, LANES), BF16)
        for e, (kk, vv) in enumerate(((klo, vlo), (khi, vhi))):
            ka_ref[e, 0:WIN, 0:LANES] = zpad
            ka_ref[e, 0:WIN, LANES:2 * LANES] = padflag
            ka_ref[e, WIN:WIN + seq, 0:LANES] = kk.astype(BF16)
            ka_ref[e, WIN:WIN + seq, LANES:2 * LANES] = zeros
            va_ref[e, 0:WIN, :] = zpad
            va_ref[e, WIN:WIN + seq, :] = vv.astype(BF16)
        r = lax.broadcasted_iota(jnp.int32, (tq, wk), 0)
        c = lax.broadcasted_iota(jnp.int32, (tq, wk), 1)
        dist = r + WIN - c
        ok = (dist >= 0) & (dist < WIN)
        distf = dist.astype(F32)
        for hg in range(A_HG):
            bw_ref[hg] = jnp.where(ok, -sl_ref[hg:hg + 1, 0:1] * distf, NEG)

    lane = lax.broadcasted_iota(jnp.int32, (tq, LANES), 1)
    negrow = jnp.where(lane == 0, NEG, 0.0).astype(BF16)
    k0 = pl.multiple_of(i * tq, tq)
    outs = []
    for p in range(2):
        qa = jnp.concatenate([q_ref[:, p * LANES:(p + 1) * LANES], negrow], axis=1)
        acc = jnp.zeros((tq, LANES), F32)
        for e in range(2):
            hg = 2 * p + e
            s = _dot_nt(qa, ka_ref[e, pl.ds(k0, wk), :]) + bw_ref[hg]
            m = jnp.max(s, axis=1, keepdims=True)
            pm = jnp.exp(s - m)
            l = jnp.sum(pm, axis=1, keepdims=True)
            col = 2 * A_HG + hg
            w = jax.nn.sigmoid(gl_ref[:, col:col + 1]) / jnp.maximum(l, 1e-30)
            acc = acc + _dot(pm.astype(BF16), va_ref[e, pl.ds(k0, wk), :]) * w
        outs.append(acc)
    o_ref[...] = jnp.concatenate(outs, axis=1).astype(o_ref.dtype)


def _win_attn(a0, gl, slopes, bsz, seq, kv_col):
    nq = seq // TQ
    return pl.pallas_call(
        _win_kernel,
        grid=(bsz, A_GROUPS, nq),
        in_specs=[
            pl.BlockSpec((TQ, 2 * LANES), lambda b, g, i: (b * nq + i, g)),
            pl.BlockSpec((seq, LANES), lambda b, g, i: (b, kv_col + g)),
            pl.BlockSpec((TQ, LANES), lambda b, g, i: (b * nq + i, g)),
            pl.BlockSpec((None, SUBLANES, LANES), lambda b, g, i: (g, 0, 0)),
        ],
        out_specs=pl.BlockSpec((TQ, 2 * LANES), lambda b, g, i: (b * nq + i, g)),
        out_shape=jax.ShapeDtypeStruct((bsz * seq, A_WIDTH), BF16),
        scratch_shapes=[
            pltpu.VMEM((2, WIN + seq, 2 * LANES), BF16),
            pltpu.VMEM((2, WIN + seq, LANES), BF16),
            pltpu.VMEM((A_HG, TQ, WIN + TQ), F32),
        ],
        compiler_params=_cparams(("parallel", "parallel", "arbitrary")),
        name="window_attention",
    )(a0, a0, gl, slopes)


def _outproj0_kernel(oc_ref, os_ref, ow_ref, z_ref, x_ref, gate_ref, w_ref, o_ref):
    o = oc_ref[...].astype(F32) + os_ref[...].astype(F32) + ow_ref[...].astype(F32)
    y = (o * _silu(z_ref[...].astype(F32))).astype(BF16)
    o_ref[...] = x_ref[...] + gate_ref[...] * _dot(y, w_ref[...])


def _outproj0(oc, osl, ow, a0, x2d, gate, w, seq, z_col):
    m, d = x2d.shape
    tm = TM_OUT
    per_b = seq // tm
    row = lambda i: (i, 0)
    return pl.pallas_call(
        _outproj0_kernel,
        grid=(m // tm,),
        in_specs=[
            pl.BlockSpec((tm, A_WIDTH), row),
            pl.BlockSpec((tm, A_WIDTH), row),
            pl.BlockSpec((tm, A_WIDTH), row),
            pl.BlockSpec((tm, A_WIDTH), lambda i: (i, z_col)),
            pl.BlockSpec((tm, d), row),
            pl.BlockSpec((None, 1, d), lambda i: (i // per_b, 0, 0)),
            pl.BlockSpec(w.shape, lambda i: (0, 0)),
        ],
        out_specs=pl.BlockSpec((tm, d), row),
        out_shape=jax.ShapeDtypeStruct((m, d), F32),
        compiler_params=_cparams(("parallel",)),
        name="nsa_out_proj",
    )(oc, osl, ow, a0, x2d, gate[:, None, :], w)


def _conv_kernel(x_ref, halo_ref, w_ref, b_ref, ks_ref, o_ref, *, per_b):
    tm = x_ref.shape[0]
    first = (pl.program_id(0) % per_b) == 0
    halo = jnp.where(first, 0.0, halo_ref[...].astype(F32))
    xe = jnp.concatenate([halo, x_ref[...].astype(F32)], axis=0)
    y = xe[SUBLANES:, :] * w_ref[CONV_W - 1:CONV_W, :] + b_ref[...]
    for s in range(1, CONV_W):
        y = y + pltpu.roll(xe, s, 0)[SUBLANES:, :] * w_ref[CONV_W - 1 - s:CONV_W - s, :]
    o_ref[...] = (_silu(y) * ks_ref[...]).astype(o_ref.dtype)


def _conv_silu(a1, conv_w, conv_b, kscale, seq):
    m = a1.shape[0]
    tm, tc = TM_CONV, TC_CONV
    per_b = seq // tm
    hb = tm // SUBLANES
    kern = functools.partial(_conv_kernel, per_b=per_b)
    return pl.pallas_call(
        kern,
        grid=(m // tm, B_QK // tc),
        in_specs=[
            pl.BlockSpec((tm, tc), lambda i, j: (i, j)),
            pl.BlockSpec((SUBLANES, tc), lambda i, j: (jnp.maximum(i * hb - 1, 0), j)),
            pl.BlockSpec((CONV_W, tc), lambda i, j: (0, j)),
            pl.BlockSpec((1, tc), lambda i, j: (0, j)),
            pl.BlockSpec((1, tc), lambda i, j: (0, j)),
        ],
        out_specs=pl.BlockSpec((tm, tc), lambda i, j: (i, j)),
        out_shape=jax.ShapeDtypeStruct((m, B_QK), BF16),
        compiler_params=_cparams(("parallel", "parallel")),
        name="causal_conv_silu",
    )(a1, a1, conv_w, conv_b.reshape(1, B_QK), kscale)


def _mlstm_kernel(q_ref, k_ref, v_ref, og_ref, z_ref, g_ref, gb_ref, hg_ref, o_ref, c_ref, n_ref, m_ref):
    ln = q_ref.shape[0]
    h = pl.program_id(1)

    @pl.when(pl.program_id(2) == 0)
    def _():
        c_ref[...] = jnp.zeros(c_ref.shape, F32)
        n_ref[...] = jnp.zeros(n_ref.shape, F32)
        m_ref[...] = jnp.zeros(m_ref.shape, F32)

    lane = lax.broadcasted_iota(jnp.int32, (ln, LANES), 1)
    gt = g_ref[...] + gb_ref[...]
    pre = jnp.where(lane < B_HEADS, gt, jnp.minimum(gt, 0.0) - jnp.log1p(jnp.exp(-jnp.abs(gt))))
    r = lax.broadcasted_iota(jnp.int32, (ln, ln), 0)
    c = lax.broadcasted_iota(jnp.int32, (ln, ln), 1)
    causal = c <= r
    cum = _dot(jnp.where(causal, 1.0, 0.0), pre, precision=HIGHEST)
    b_col = jnp.sum(jnp.where(lane == B_HEADS + h, cum, 0.0), axis=1, keepdims=True)
    i_col = jnp.sum(jnp.where(lane == h, pre, 0.0), axis=1, keepdims=True)
    sub = lax.broadcasted_iota(jnp.int32, (LANES, ln), 0)
    b_row = jnp.sum(jnp.where(sub == B_HEADS + h, cum.T, 0.0), axis=0, keepdims=True)
    i_row = jnp.sum(jnp.where(sub == h, pre.T, 0.0), axis=0, keepdims=True)
    g_tot = b_col[ln - 1:ln, :]
    m_st = m_ref[...]

    q = q_ref[...]
    k = k_ref[...]
    v = v_ref[...]
    log_d = jnp.where(causal, b_col - b_row + i_row, NEG)
    m_inter = b_col + m_st
    m_j = jnp.maximum(jnp.max(log_d, axis=1, keepdims=True), m_inter)
    smat = _dot_nt(q, k) * jnp.exp(log_d - m_j)
    w_inter = jnp.exp(m_inter - m_j)
    num = _dot(smat.astype(BF16), v) + w_inter * _dot(q, c_ref[...].astype(BF16))
    qn = jnp.sum(q.astype(F32) * n_ref[...], axis=1, keepdims=True)
    den = jnp.sum(smat, axis=1, keepdims=True) + w_inter * qn
    hh = num / jnp.maximum(jnp.abs(den), jnp.exp(-m_j))

    a_col = g_tot - b_col + i_col
    m_loc = jnp.max(a_col, axis=0, keepdims=True)
    kw = k.astype(F32) * jnp.exp(a_col - m_loc)
    c_loc = _dot(kw.T.astype(BF16), v)
    n_loc = jnp.sum(kw, axis=0, keepdims=True)
    m_new = jnp.maximum(g_tot + m_st, m_loc)
    s_old = jnp.exp(g_tot + m_st - m_new)
    s_new = jnp.exp(m_loc - m_new)
    c_ref[...] = s_old * c_ref[...] + s_new * c_loc
    n_ref[...] = s_old * n_ref[...] + s_new * n_loc
    m_ref[...] = m_new

    hn = hh * lax.rsqrt(jnp.mean(hh * hh, axis=-1, keepdims=True) + EPS) * hg_ref[...]
    o_ref[...] = (jax.nn.sigmoid(og_ref[...].astype(F32)) * hn * _silu(z_ref[...].astype(F32))).astype(o_ref.dtype)


def _mlstm(qk, a1, g1, gbias, head_g, bsz, seq):
    ln = ML_CHUNK
    nc = seq // ln
    vcol = B_QK // B_DV
    nv = B_WIDTH // B_DV
    return pl.pallas_call(
        _mlstm_kernel,
        grid=(bsz, B_HEADS, nc),
        in_specs=[
            pl.BlockSpec((ln, B_DK), lambda b, h, n: (b * nc + n, h)),
            pl.BlockSpec((ln, B_DK), lambda b, h, n: (b * nc + n, B_HEADS + h)),
            pl.BlockSpec((ln, B_DV), lambda b, h, n: (b * nc + n, vcol + h)),
            pl.BlockSpec((ln, B_DV), lambda b, h, n: (b * nc + n, vcol + nv + h)),
            pl.BlockSpec((ln, B_DV), lambda b, h, n: (b * nc + n, vcol + 2 * nv + h)),
            pl.BlockSpec((ln, LANES), lambda b, h, n: (b * nc + n, 0)),
            pl.BlockSpec((1, LANES), lambda b, h, n: (0, 0)),
            pl.BlockSpec((1, B_DV), lambda b, h, n: (0, h)),
        ],
        out_specs=pl.BlockSpec((ln, B_DV), lambda b, h, n: (b * nc + n, h)),
        out_shape=jax.ShapeDtypeStruct((bsz * seq, B_WIDTH), BF16),
        scratch_shapes=[
            pltpu.VMEM((B_DK, B_DV), F32),
            pltpu.VMEM((1, B_DK), F32),
            pltpu.VMEM((1, 1), F32),
        ],
        compiler_params=_cparams(("parallel", "parallel", "arbitrary")),
        name="mlstm_chunkwise",
    )(qk, qk, a1, a1, a1, g1, gbias, head_g.reshape(1, B_WIDTH))


def _outproj1_kernel(y_ref, x_ref, gate_ref, w_ref, fg_ref, o_ref):
    x2 = x_ref[...] + gate_ref[...] * _dot(y_ref[...], w_ref[...])
    o_ref[...] = x2 * lax.rsqrt(jnp.mean(x2 * x2, axis=-1, keepdims=True) + EPS) * fg_ref[...]


def _outproj1(y, x2d, gate, w, final_g, seq):
    m, d = x2d.shape
    tm = TM_OUT
    per_b = seq // tm
    row = lambda i: (i, 0)
    return pl.pallas_call(
        _outproj1_kernel,
        grid=(m // tm,),
        in_specs=[
            pl.BlockSpec((tm, y.shape[1]), row),
            pl.BlockSpec((tm, d), row),
            pl.BlockSpec((None, 1, d), lambda i: (i // per_b, 0, 0)),
            pl.BlockSpec(w.shape, lambda i: (0, 0)),
            pl.BlockSpec((1, d), lambda i: (0, 0)),
        ],
        out_specs=pl.BlockSpec((tm, d), row),
        out_shape=jax.ShapeDtypeStruct((m, d), F32),
        compiler_params=_cparams(("parallel",)),
        name="mlstm_out_proj_final_norm",
    )(y, x2d, gate[:, None, :], w, final_g.reshape(1, d))


def _alibi_slopes():
    return np.asarray(2.0 ** (-8.0 * np.arange(1, A_HEADS + 1) / A_HEADS), np.float32)


def _selection_overlap(seq, rows):
    n_cmp = seq // CMP_STRIDE - CMP_LEN // CMP_STRIDE + 1
    n_sel = seq // SLC_LEN
    c0 = np.arange(n_cmp)[:, None] * CMP_STRIDE
    s0 = np.arange(n_sel)[None, :] * SLC_LEN
    ov = np.clip(np.minimum(c0 + CMP_LEN, s0 + SLC_LEN) - np.maximum(c0, s0), 0, None) / CMP_LEN
    out = np.zeros((rows, LANES), np.float32)
    out[:n_cmp, :n_sel] = ov
    return out


def _layer0_weights(w_in):
    d = w_in.shape[0]
    kv0 = A_WIDTH
    gl0 = kv0 + 6 * A_GROUPS * A_DH
    z0 = gl0 + 3 * A_HEADS
    wq = w_in[:, :A_WIDTH] * (A_DH ** -0.5)
    wz = w_in[:, z0:z0 + A_WIDTH]
    wkv = w_in[:, kv0:gl0].reshape(d, 6, A_GROUPS, A_DH)
    branches = [wkv[:, 2 * r:2 * r + 2].transpose(0, 2, 1, 3).reshape(d, A_GROUPS * 2 * A_DH) for r in range(3)]
    w0 = jnp.concatenate([wq, wz] + branches, axis=1).astype(BF16)
    wgl = w_in[:, gl0:z0].reshape(d, 3, A_GROUPS, A_HG).transpose(0, 2, 1, 3).reshape(d, A_GROUPS, 3 * A_HG)
    wgl = jnp.pad(wgl, ((0, 0), (0, 0), (0, LANES - 3 * A_HG))).reshape(d, A_GROUPS * LANES).astype(BF16)
    return w0, wgl


def _compress_weights(pe, w1, b1, w2, b2):
    half = CMP_LEN // 2
    w1r = w1.reshape(2, 2, half, A_DH, CMP_HIDDEN)
    wc = jnp.einsum("khldn,kq->lkdqhn", w1r, jnp.eye(2, dtype=w1.dtype))
    wc = wc.reshape(half * 2 * A_DH, 4 * CMP_HIDDEN).astype(BF16)
    pec = pe.reshape(2, 2, half, A_DH).transpose(1, 2, 0, 3).reshape(2, half * 2 * A_DH)
    pec = jnp.pad(pec, ((0, SUBLANES - 2), (0, 0))).astype(BF16)
    b1c = b1.reshape(1, 2 * CMP_HIDDEN)
    w2a = jnp.zeros((2 * CMP_HIDDEN, 4 * LANES), F32)
    b2a = jnp.zeros((1, 4 * LANES), F32)
    for kv in range(2):
        for hi in range(2):
            c0 = (2 * kv + hi) * LANES + hi * A_DH
            w2a = w2a.at[kv * CMP_HIDDEN:(kv + 1) * CMP_HIDDEN, c0:c0 + A_DH].set(w2[kv])
            b2a = b2a.at[0, c0:c0 + A_DH].set(b2[kv])
    return wc, pec, b1c, w2a.astype(BF16), b2a


def _layer1_weights(w_in):
    d = w_in.shape[0]
    g0 = B_QK + B_WIDTH
    w_main = jnp.concatenate([w_in[:, :g0], w_in[:, g0 + 2 * B_HEADS:]], axis=1).astype(BF16)
    wg = jnp.pad(w_in[:, g0:g0 + 2 * B_HEADS], ((0, 0), (0, LANES - 2 * B_HEADS))).astype(BF16)
    return w_main, wg


def kernel(x, c, ada_w, ada_b, norm_g, final_g, a_w_in, a_cmp_pe, a_cmp_w1, a_cmp_b1, a_cmp_w2, a_cmp_b2,
           a_w_out, b_w_in, b_conv_w, b_conv_b, b_gate_b, b_head_g, b_w_out):
    bsz, seq, d = x.shape
    assert d == D_MODEL and seq % (2 * TM_PROJ) == 0 and ada_w.shape[0] == 2
    m = bsz * seq
    x2d = x.reshape(m, d)
    mod = _ada_mod(c, ada_w, ada_b)

    w0, wgl = _layer0_weights(a_w_in[0])
    a0, gl = _normmod_matmul(x2d, norm_g[0], mod[0, :, d:2 * d], mod[0, :, :d], w0, wgl, seq, "nsa_in_proj")
    z_col = 1
    kv_col = 2 * A_WIDTH // LANES
    n_chunk = seq // CMP_STRIDE
    n_cmp = n_chunk - CMP_LEN // CMP_STRIDE + 1
    n_sel = seq // SLC_LEN
    n_top = min(SLC_TOP, n_sel)
    kvsrc = a0[:, 2 * A_WIDTH:2 * A_WIDTH + A_GROUPS * LANES]
    xc = kvsrc.reshape(bsz, n_chunk, CMP_STRIDE, A_GROUPS, LANES).transpose(0, 3, 1, 2, 4)
    xc = xc.reshape(bsz, A_GROUPS, n_chunk, CMP_STRIDE * LANES)
    kvc = _compress(xc, *_compress_weights(a_cmp_pe[0], a_cmp_w1[0], a_cmp_b1[0], a_cmp_w2[0], a_cmp_b2[0]))
    slopes = jnp.asarray(np.broadcast_to(
        np.pad(_alibi_slopes().reshape(A_GROUPS, A_HG), ((0, 0), (0, SUBLANES - A_HG)))[:, :, None],
        (A_GROUPS, SUBLANES, LANES)))
    ov = jnp.asarray(_selection_overlap(seq, n_chunk))
    o_cmp, sb = _cmp_attn(a0, kvc, gl, slopes, ov, bsz, seq, n_cmp, n_sel, n_top)
    o_slc = _slc_attn(a0, sb, gl, slopes, bsz, seq, kv_col + A_GROUPS)
    o_win = _win_attn(a0, gl, slopes, bsz, seq, kv_col + 2 * A_GROUPS)
    x1 = _outproj0(o_cmp, o_slc, o_win, a0, x2d, mod[0, :, 2 * d:], a_w_out[0].astype(BF16), seq, z_col)

    w1m, w1g = _layer1_weights(b_w_in[0])
    a1, g1 = _normmod_matmul(x1, norm_g[1], mod[1, :, d:2 * d], mod[1, :, :d], w1m, w1g, seq, "mlstm_in_proj")
    kscale = jnp.concatenate([jnp.ones((1, B_QK // 2), F32), jnp.full((1, B_QK // 2), B_DK ** -0.5, F32)], axis=1)
    qk = _conv_silu(a1, b_conv_w[0], b_conv_b[0], kscale, seq)
    gbias = jnp.pad(b_gate_b[0].reshape(1, 2 * B_HEADS), ((0, 0), (0, LANES - 2 * B_HEADS)))
    y1 = _mlstm(qk, a1, g1, gbias, b_head_g[0], bsz, seq)
    out = _outproj1(y1, x1, mod[1, :, 2 * d:], b_w_out[0].astype(BF16), final_g, seq)
    return out.reshape(bsz, seq, d)
```

```python
import functools

import numpy as np
import jax
import jax.numpy as jnp
from jax import lax
from jax.experimental import pallas as pl
from jax.experimental.pallas import tpu as pltpu

F32 = jnp.float32
BF16 = jnp.bfloat16
HIGHEST = lax.Precision.HIGHEST

EPS = 1e-6
NEG = -1e30
LOG2E = float(np.log2(np.e))

D_MODEL = 1024
A_HEADS = 16
A_GROUPS = 4
A_HG = A_HEADS // A_GROUPS
A_DH = 64
A_WIDTH = A_HEADS * A_DH
CMP_LEN = 32
CMP_STRIDE = 16
CMP_HIDDEN = 256
SLC_LEN = 64
SLC_SHIFT = 6
SLC_TOP = 16
WIN = 512
B_HEADS = 8
B_DK = 128
B_DV = 256
B_WIDTH = B_HEADS * B_DV
B_QK = 2 * B_HEADS * B_DK
CONV_W = 4

LANES = 128
SUBLANES = 8
VMEM_LIMIT = 56 * 1024 * 1024

TQ = 256
ML_CHUNK = 256
TM_PROJ = 1024
TN_PROJ = 512
TM_OUT = 512
TM_CONV = 1024
TC_CONV = 512

assert WIN % TQ == 0 and 1 << SLC_SHIFT == SLC_LEN


def _dot(a, b, **kw):
    return jnp.dot(a, b, preferred_element_type=F32, **kw)


def _dot_nt(a, b):
    return lax.dot_general(a, b, (((1,), (1,)), ((), ())), preferred_element_type=F32)


def _cparams(sem):
    return pltpu.CompilerParams(dimension_semantics=sem, vmem_limit_bytes=VMEM_LIMIT)


def _silu(x):
    return x * jax.nn.sigmoid(x)


def _ada_kernel(c_ref, w_ref, b_ref, o_ref):
    o_ref[...] = _dot(c_ref[...], w_ref[...], precision=HIGHEST) + b_ref[...]


def _ada_mod(c, ada_w, ada_b):
    depth, d, d3 = ada_w.shape
    bsz = c.shape[0]
    nj = d3 // d
    return pl.pallas_call(
        _ada_kernel,
        grid=(depth, nj),
        in_specs=[
            pl.BlockSpec((bsz, d), lambda i, j: (0, 0)),
            pl.BlockSpec((None, d, d), lambda i, j: (i, 0, j)),
            pl.BlockSpec((None, 1, d), lambda i, j: (i, 0, j)),
        ],
        out_specs=pl.BlockSpec((None, bsz, d), lambda i, j: (i, 0, j)),
        out_shape=jax.ShapeDtypeStruct((depth, bsz, d3), F32),
        compiler_params=_cparams(("parallel", "parallel")),
        name="ada_mod",
    )(c, ada_w, ada_b.reshape(depth, 1, d3))


def _normmod_kernel(x_ref, g_ref, sc_ref, sh_ref, w_ref, wg_ref, o_ref, og_ref, h_ref):
    @pl.when(pl.program_id(1) == 0)
    def _():
        x = x_ref[...]
        xn = x * lax.rsqrt(jnp.mean(x * x, axis=-1, keepdims=True) + EPS)
        h = (xn * g_ref[...]) * (1.0 + sc_ref[...]) + sh_ref[...]
        hb = h.astype(BF16)
        h_ref[...] = hb
        og_ref[...] = _dot(hb, wg_ref[...])

    o_ref[...] = _dot(h_ref[...], w_ref[...]).astype(o_ref.dtype)


def _normmod_matmul(x2d, g, scale, shift, w, wg, seq, name):
    m, d = x2d.shape
    n = w.shape[1]
    ng = wg.shape[1]
    tm, tn = TM_PROJ, TN_PROJ
    assert m % tm == 0 and n % tn == 0 and seq % tm == 0
    per_b = seq // tm
    return pl.pallas_call(
        _normmod_kernel,
        grid=(m // tm, n // tn),
        in_specs=[
            pl.BlockSpec((tm, d), lambda i, j: (i, 0)),
            pl.BlockSpec((1, d), lambda i, j: (0, 0)),
            pl.BlockSpec((None, 1, d), lambda i, j: (i // per_b, 0, 0)),
            pl.BlockSpec((None, 1, d), lambda i, j: (i // per_b, 0, 0)),
            pl.BlockSpec((d, tn), lambda i, j: (0, j)),
            pl.BlockSpec((d, ng), lambda i, j: (0, 0)),
        ],
        out_specs=[
            pl.BlockSpec((tm, tn), lambda i, j: (i, j)),
            pl.BlockSpec((tm, ng), lambda i, j: (i, 0)),
        ],
        out_shape=[jax.ShapeDtypeStruct((m, n), BF16), jax.ShapeDtypeStruct((m, ng), F32)],
        scratch_shapes=[pltpu.VMEM((tm, d), BF16)],
        compiler_params=_cparams(("parallel", "arbitrary")),
        name=name,
    )(x2d, g.reshape(1, d), scale[:, None, :], shift[:, None, :], w, wg)


def _compress_kernel(x_ref, wc_ref, pe_ref, b1_ref, w2k_ref, b2k_ref, w2vt_ref, b2v_ref, ok_ref, ov_ref):
    a = _dot(x_ref[...], wc_ref[...])
    pb = _dot(pe_ref[...], wc_ref[...])
    n = a.shape[0]
    hid = CMP_HIDDEN
    hs = []
    for kv in range(2):
        c0 = kv * 2 * hid
        first = a[:, c0:c0 + hid]
        second = pltpu.roll(a[:, c0 + hid:c0 + 2 * hid], n - 1, 0)
        bias = pb[0:1, c0:c0 + hid] + pb[1:2, c0 + hid:c0 + 2 * hid] + b1_ref[:, kv * hid:(kv + 1) * hid]
        hs.append(jax.nn.gelu(first + second + bias))
    ok_ref[...] = (_dot(hs[0].astype(BF16), w2k_ref[...]) + b2k_ref[...]).astype(ok_ref.dtype)
    ov_ref[...] = (_dot_nt(w2vt_ref[...], hs[1].astype(BF16)) + b2v_ref[:, 0:1]).astype(ov_ref.dtype)


def _compress(xc, wc, pec, b1c, w2k, b2k, w2vt, b2v):
    bsz, g, nchunk, kdim = xc.shape
    full = lambda a: pl.BlockSpec(a.shape, lambda b, gi: (0,) * a.ndim)
    return pl.pallas_call(
        _compress_kernel,
        grid=(bsz, g),
        in_specs=[pl.BlockSpec((None, None, nchunk, kdim), lambda b, gi: (b, gi, 0, 0)),
                  full(wc), full(pec), full(b1c), full(w2k), full(b2k), full(w2vt), full(b2v)],
        out_specs=[
            pl.BlockSpec((None, None, nchunk, 2 * LANES), lambda b, gi: (b, gi, 0, 0)),
            pl.BlockSpec((None, None, A_DH, nchunk), lambda b, gi: (b, gi, 0, 0)),
        ],
        out_shape=[
            jax.ShapeDtypeStruct((bsz, g, nchunk, 2 * LANES), BF16),
            jax.ShapeDtypeStruct((bsz, g, A_DH, nchunk), BF16),
        ],
        compiler_params=_cparams(("parallel", "parallel")),
        name="compress_tokens",
    )(xc, wc, pec, b1c, w2k, b2k, w2vt, b2v)


def _cmp_attn_kernel(q_ref, k_ref, vt_ref, gl_ref, sl_ref, ovt_ref, o_ref, sb_ref, *, n_cmp, n_sel, n_top):
    tq = q_ref.shape[0]
    ncol = k_ref.shape[0]
    q0 = pl.program_id(2) * tq
    tpos = lax.broadcasted_iota(jnp.int32, (ncol, tq), 1) + q0
    cidx = lax.broadcasted_iota(jnp.int32, (ncol, tq), 0)
    dist = tpos - (cidx * CMP_STRIDE + (CMP_LEN - 1))
    valid = (dist >= 0) & (cidx < n_cmp)
    distf = dist.astype(F32)
    gate_t = jax.nn.sigmoid(gl_ref[...]).T
    vt = vt_ref[...]
    psum = jnp.zeros((ncol, tq), F32)
    scores = [_dot_nt(k_ref[:, (hg % 2) * LANES:(hg % 2 + 1) * LANES], q_ref[:, (hg // 2) * LANES:(hg // 2 + 1) * LANES])
              for hg in range(A_HG)]
    probs = []
    for hg in range(A_HG):
        s = jnp.where(valid, scores[hg] - sl_ref[hg:hg + 1, 0:1] * distf, NEG)
        m = jnp.max(s, axis=0, keepdims=True)
        m = jnp.where(m > 0.5 * NEG, m, 0.0)
        pr = jnp.exp2(s - m)
        pr = pr * (1.0 / jnp.maximum(jnp.sum(pr, axis=0, keepdims=True), 1e-30))
        psum = psum + pr
        probs.append(pr.astype(BF16))
    outs = [_dot(vt, probs[hg]) * gate_t[hg:hg + 1, :] for hg in range(A_HG)]
    o_ref[...] = jnp.concatenate(outs, axis=0).T.astype(o_ref.dtype)

    imp = _dot(ovt_ref[...], psum, precision=HIGHEST)
    blk = lax.broadcasted_iota(jnp.int32, (n_sel, tq), 0)
    blkf = blk.astype(F32)
    cur = lax.shift_right_logical(lax.broadcasted_iota(jnp.int32, (n_sel, tq), 1) + q0, SLC_SHIFT)
    forced = (blk == 0) | (blk == cur) | (blk == cur - 1)
    score = jnp.where(forced, 3e38, jnp.where(blk <= cur, imp, -1.0))
    sel = jnp.zeros((n_sel, tq), F32)
    for _ in range(n_top):
        mx = jnp.max(score, axis=0, keepdims=True)
        idx = jnp.min(jnp.where(score == mx, blkf, 1e9), axis=0, keepdims=True)
        pick = blkf == idx
        sel = jnp.where(pick, 1.0, sel)
        score = jnp.where(pick, -3.0, score)
    bias_t = jnp.where(sel > 0.0, 0.0, NEG)
    bias_t = jnp.concatenate([bias_t, jnp.zeros((LANES - n_sel, tq), F32)], axis=0)
    sb_ref[...] = bias_t.T.astype(sb_ref.dtype)


def _cmp_attn(a0, kc, vct, gl, slopes, ovt, bsz, seq, n_cmp, n_sel, n_top):
    nq = seq // TQ
    ncol = kc.shape[2]
    kern = functools.partial(_cmp_attn_kernel, n_cmp=n_cmp, n_sel=n_sel, n_top=n_top)
    return pl.pallas_call(
        kern,
        grid=(bsz, A_GROUPS, nq),
        in_specs=[
            pl.BlockSpec((TQ, 2 * LANES), lambda b, g, i: (b * nq + i, g)),
            pl.BlockSpec((None, None, ncol, 2 * LANES), lambda b, g, i: (b, g, 0, 0)),
            pl.BlockSpec((None, None, A_DH, ncol), lambda b, g, i: (b, g, 0, 0)),
            pl.BlockSpec((TQ, LANES), lambda b, g, i: (b * nq + i, g)),
            pl.BlockSpec((None, SUBLANES, LANES), lambda b, g, i: (g, 0, 0)),
            pl.BlockSpec(ovt.shape, lambda b, g, i: (0, 0)),
        ],
        out_specs=[
            pl.BlockSpec((TQ, 2 * LANES), lambda b, g, i: (b * nq + i, g)),
            pl.BlockSpec((None, None, TQ, LANES), lambda b, g, i: (b, g, i, 0)),
        ],
        out_shape=[
            jax.ShapeDtypeStruct((bsz * seq, A_WIDTH), BF16),
            jax.ShapeDtypeStruct((bsz, A_GROUPS, seq, LANES), BF16),
        ],
        compiler_params=_cparams(("parallel", "parallel", "parallel")),
        name="cmp_attention_topk",
    )(a0, kc, vct, gl, slopes, ovt)


def _stage_kv(kv_ref, ka_ref, vt_ref, row0, chunk, aug_fn):
    seq = kv_ref.shape[0]
    for t in range(seq // chunk):
        kvf = kv_ref[t * chunk:(t + 1) * chunk, :].astype(F32)
        low = lax.broadcasted_iota(jnp.int32, kvf.shape, 1) < A_DH
        rolled = pltpu.roll(kvf, A_DH, 1)
        r0 = row0 + t * chunk
        ka_ref[0, r0:r0 + chunk, 0:LANES] = jnp.where(low, kvf, 0.0).astype(BF16)
        ka_ref[1, r0:r0 + chunk, 0:LANES] = jnp.where(low, 0.0, rolled).astype(BF16)
        aug = aug_fn(t)
        ka_ref[0, r0:r0 + chunk, LANES:2 * LANES] = aug
        ka_ref[1, r0:r0 + chunk, LANES:2 * LANES] = aug
        vt_ref[row0 // chunk + t] = kvf.T[A_DH:, :].astype(BF16)


def _slc_kernel(q_ref, sb_ref, kv_ref, gl_ref, sl_ref, o_ref,
                ka_ref, vt_ref, sd_ref, bd_ref, m_ref, l_ref, acc_ref, qa_ref, sa_ref, sb2_ref):
    tq = q_ref.shape[0]
    tk = tq
    i = pl.program_id(2)

    @pl.when(i == 0)
    def _stage():
        def onehot(t):
            blk = lax.shift_right_logical(lax.broadcasted_iota(jnp.int32, (tk, LANES), 0) + t * tk, SLC_SHIFT)
            return jnp.where(blk == lax.broadcasted_iota(jnp.int32, (tk, LANES), 1), 1.0, 0.0).astype(BF16)

        _stage_kv(kv_ref, ka_ref, vt_ref, 0, tk, onehot)
        kk = lax.broadcasted_iota(jnp.int32, (tk, tq), 0)
        qq = lax.broadcasted_iota(jnp.int32, (tk, tq), 1)
        d = (qq - kk).astype(F32)
        for hg in range(A_HG):
            sd = -sl_ref[hg:hg + 1, 0:1] * d
            sd_ref[hg] = sd
            bd_ref[hg] = jnp.where(kk <= qq, sd, NEG)

    sb = sb_ref[...]
    for p in range(2):
        qa_ref[p, :, 0:LANES] = q_ref[:, p * LANES:(p + 1) * LANES]
        qa_ref[p, :, LANES:2 * LANES] = sb
    m_ref[...] = jnp.full(m_ref.shape, NEG, F32)
    l_ref[...] = jnp.zeros(l_ref.shape, F32)
    acc_ref[...] = jnp.zeros(acc_ref.shape, F32)

    def scores(kt, s_ref):
        k0 = pl.multiple_of(kt * tk, tk)
        for hg in range(A_HG):
            s_ref[hg] = _dot_nt(ka_ref[hg % 2, pl.ds(k0, tk), :], qa_ref[hg // 2])

    def softmax_pv(kt, s_ref, bias_ref):
        off = ((i - kt) * tk).astype(F32)
        vt = vt_ref[kt]
        m_all = m_ref[...]
        l_all = l_ref[...]
        probs, alphas, m_rows, l_rows = [], [], [], []
        for hg in range(A_HG):
            cst = -sl_ref[hg:hg + 1, 0:1] * off
            s = s_ref[hg] + bias_ref[hg]
            m_prev = m_all[hg:hg + 1, :]
            m_new = jnp.maximum(m_prev, jnp.max(s, axis=0, keepdims=True) + cst)
            alpha = jnp.exp2(m_prev - m_new)
            pm = jnp.exp2(s - (m_new - cst))
            l_rows.append(alpha * l_all[hg:hg + 1, :] + jnp.sum(pm, axis=0, keepdims=True))
            m_rows.append(m_new)
            alphas.append(alpha)
            probs.append(pm.astype(BF16))
        pad = [jnp.zeros((SUBLANES - A_HG, tq), F32)]
        m_ref[...] = jnp.concatenate(m_rows + pad, axis=0)
        l_ref[...] = jnp.concatenate(l_rows + pad, axis=0)
        for hg in range(A_HG):
            acc_ref[hg] = acc_ref[hg] * alphas[hg] + _dot(vt, probs[hg])

    scores(0, sa_ref)

    def body(j, carry):
        kt = 2 * j
        scores(kt + 1, sb2_ref)
        softmax_pv(kt, sa_ref, sd_ref)
        scores(kt + 2, sa_ref)
        softmax_pv(kt + 1, sb2_ref, sd_ref)
        return carry

    lax.fori_loop(0, i // 2, body, 0)

    @pl.when(i % 2 == 0)
    def _():
        softmax_pv(i, sa_ref, bd_ref)

    @pl.when(i % 2 == 1)
    def _():
        scores(i, sb2_ref)
        softmax_pv(i - 1, sa_ref, sd_ref)
        softmax_pv(i, sb2_ref, bd_ref)

    gate_t = jax.nn.sigmoid(gl_ref[...]).T
    outs = []
    for hg in range(A_HG):
        w = gate_t[A_HG + hg:A_HG + hg + 1, :] / jnp.maximum(l_ref[hg:hg + 1, :], 1e-30)
        outs.append(acc_ref[hg] * w)
    o_ref[...] = jnp.concatenate(outs, axis=0).T.astype(o_ref.dtype)


def _slc_attn(a0, sb, gl, slopes, bsz, seq, kv_col):
    nq = seq // TQ
    return pl.pallas_call(
        _slc_kernel,
        grid=(bsz, A_GROUPS, nq),
        in_specs=[
            pl.BlockSpec((TQ, 2 * LANES), lambda b, g, i: (b * nq + i, g)),
            pl.BlockSpec((None, None, TQ, LANES), lambda b, g, i: (b, g, i, 0)),
            pl.BlockSpec((seq, LANES), lambda b, g, i: (b, kv_col + g)),
            pl.BlockSpec((TQ, LANES), lambda b, g, i: (b * nq + i, g)),
            pl.BlockSpec((None, SUBLANES, LANES), lambda b, g, i: (g, 0, 0)),
        ],
        out_specs=pl.BlockSpec((TQ, 2 * LANES), lambda b, g, i: (b * nq + i, g)),
        out_shape=jax.ShapeDtypeStruct((bsz * seq, A_WIDTH), BF16),
        scratch_shapes=[
            pltpu.VMEM((2, seq, 2 * LANES), BF16),
            pltpu.VMEM((seq // TQ, A_DH, TQ), BF16),
            pltpu.VMEM((A_HG, TQ, TQ), F32),
            pltpu.VMEM((A_HG, TQ, TQ), F32),
            pltpu.VMEM((SUBLANES, TQ), F32),
            pltpu.VMEM((SUBLANES, TQ), F32),
            pltpu.VMEM((A_HG, A_DH, TQ), F32),
            pltpu.VMEM((2, TQ, 2 * LANES), BF16),
            pltpu.VMEM((A_HG, TQ, TQ), F32),
            pltpu.VMEM((A_HG, TQ, TQ), F32),
        ],
        compiler_params=_cparams(("parallel", "parallel", "arbitrary")),
        name="selected_attention",
    )(a0, sb, a0, gl, slopes)


def _win_kernel(q_ref, kv_ref, gl_ref, sl_ref, o_ref, ka_ref, vt_ref, bw_ref):
    tq = q_ref.shape[0]
    wk = WIN + tq
    npad = WIN // tq
    i = pl.program_id(2)

    @pl.when(i == 0)
    def _stage():
        lane_p = lax.broadcasted_iota(jnp.int32, (WIN, LANES), 1)
        for e in range(2):
            ka_ref[e, 0:WIN, 0:LANES] = jnp.zeros((WIN, LANES), BF16)
            ka_ref[e, 0:WIN, LANES:2 * LANES] = jnp.where(lane_p == 0, 1.0, 0.0).astype(BF16)
        for t in range(npad):
            vt_ref[t] = jnp.zeros(vt_ref.shape[1:], BF16)
        _stage_kv(kv_ref, ka_ref, vt_ref, WIN, tq, lambda t: jnp.zeros((tq, LANES), BF16))
        kk = lax.broadcasted_iota(jnp.int32, (wk, tq), 0)
        qq = lax.broadcasted_iota(jnp.int32, (wk, tq), 1)
        dist = qq + WIN - kk
        ok = (dist >= 0) & (dist < WIN)
        distf = dist.astype(F32)
        for hg in range(A_HG):
            bw_ref[hg] = jnp.where(ok, -sl_ref[hg:hg + 1, 0:1] * distf, NEG)

    lane = lax.broadcasted_iota(jnp.int32, (tq, LANES), 1)
    negrow = jnp.where(lane == 0, NEG, 0.0).astype(BF16)
    k0 = pl.multiple_of(i * tq, tq)
    gate_t = jax.nn.sigmoid(gl_ref[...]).T
    qa = [jnp.concatenate([q_ref[:, p * LANES:(p + 1) * LANES], negrow], axis=1) for p in range(2)]
    scores = [_dot_nt(ka_ref[hg % 2, pl.ds(k0, wk), :], qa[hg // 2]) for hg in range(A_HG)]
    probs, scales = [], []
    for hg in range(A_HG):
        s = scores[hg] + bw_ref[hg]
        pm = jnp.exp2(s - jnp.max(s, axis=0, keepdims=True))
        l = jnp.sum(pm, axis=0, keepdims=True)
        scales.append(gate_t[2 * A_HG + hg:2 * A_HG + hg + 1, :] / jnp.maximum(l, 1e-30))
        probs.append(pm.astype(BF16))
    outs = []
    for hg in range(A_HG):
        o = _dot(vt_ref[i], probs[hg][0:tq, :])
        for j in range(1, wk // tq):
            o = o + _dot(vt_ref[i + j], probs[hg][j * tq:(j + 1) * tq, :])
        outs.append(o * scales[hg])
    o_ref[...] = jnp.concatenate(outs, axis=0).T.astype(o_ref.dtype)


def _win_attn(a0, gl, slopes, bsz, seq, kv_col):
    nq = seq // TQ
    return pl.pallas_call(
        _win_kernel,
        grid=(bsz, A_GROUPS, nq),
        in_specs=[
            pl.BlockSpec((TQ, 2 * LANES), lambda b, g, i: (b * nq + i, g)),
            pl.BlockSpec((seq, LANES), lambda b, g, i: (b, kv_col + g)),
            pl.BlockSpec((TQ, LANES), lambda b, g, i: (b * nq + i, g)),
            pl.BlockSpec((None, SUBLANES, LANES), lambda b, g, i: (g, 0, 0)),
        ],
        out_specs=pl.BlockSpec((TQ, 2 * LANES), lambda b, g, i: (b * nq + i, g)),
        out_shape=jax.ShapeDtypeStruct((bsz * seq, A_WIDTH), BF16),
        scratch_shapes=[
            pltpu.VMEM((2, WIN + seq, 2 * LANES), BF16),
            pltpu.VMEM(((WIN + seq) // TQ, A_DH, TQ), BF16),
            pltpu.VMEM((A_HG, WIN + TQ, TQ), F32),
        ],
        compiler_params=_cparams(("parallel", "parallel", "arbitrary")),
        name="window_attention",
    )(a0, a0, gl, slopes)


def _outproj0_kernel(oc_ref, os_ref, ow_ref, z_ref, x_ref, gate_ref, w_ref, o_ref):
    o = oc_ref[...].astype(F32) + os_ref[...].astype(F32) + ow_ref[...].astype(F32)
    y = (o * _silu(z_ref[...].astype(F32))).astype(BF16)
    o_ref[...] = x_ref[...] + gate_ref[...] * _dot(y, w_ref[...])


def _outproj0(oc, osl, ow, a0, x2d, gate, w, seq, z_col):
    m, d = x2d.shape
    tm = TM_OUT
    per_b = seq // tm
    row = lambda i: (i, 0)
    return pl.pallas_call(
        _outproj0_kernel,
        grid=(m // tm,),
        in_specs=[
            pl.BlockSpec((tm, A_WIDTH), row),
            pl.BlockSpec((tm, A_WIDTH), row),
            pl.BlockSpec((tm, A_WIDTH), row),
            pl.BlockSpec((tm, A_WIDTH), lambda i: (i, z_col)),
            pl.BlockSpec((tm, d), row),
            pl.BlockSpec((None, 1, d), lambda i: (i // per_b, 0, 0)),
            pl.BlockSpec(w.shape, lambda i: (0, 0)),
        ],
        out_specs=pl.BlockSpec((tm, d), row),
        out_shape=jax.ShapeDtypeStruct((m, d), F32),
        compiler_params=_cparams(("parallel",)),
        name="nsa_out_proj",
    )(oc, osl, ow, a0, x2d, gate[:, None, :], w)


def _conv_kernel(x_ref, halo_ref, w_ref, b_ref, ks_ref, o_ref, *, per_b):
    first = (pl.program_id(0) % per_b) == 0
    halo = jnp.where(first, 0.0, halo_ref[...].astype(F32))
    xe = jnp.concatenate([halo, x_ref[...].astype(F32)], axis=0)
    y = xe[SUBLANES:, :] * w_ref[CONV_W - 1:CONV_W, :] + b_ref[...]
    for s in range(1, CONV_W):
        y = y + pltpu.roll(xe, s, 0)[SUBLANES:, :] * w_ref[CONV_W - 1 - s:CONV_W - s, :]
    o_ref[...] = (_silu(y) * ks_ref[...]).astype(o_ref.dtype)


def _conv_silu(a1, conv_w, conv_b, kscale, seq):
    m = a1.shape[0]
    tm, tc = TM_CONV, TC_CONV
    per_b = seq // tm
    hb = tm // SUBLANES
    kern = functools.partial(_conv_kernel, per_b=per_b)
    return pl.pallas_call(
        kern,
        grid=(m // tm, B_QK // tc),
        in_specs=[
            pl.BlockSpec((tm, tc), lambda i, j: (i, j)),
            pl.BlockSpec((SUBLANES, tc), lambda i, j: (jnp.maximum(i * hb - 1, 0), j)),
            pl.BlockSpec((CONV_W, tc), lambda i, j: (0, j)),
            pl.BlockSpec((1, tc), lambda i, j: (0, j)),
            pl.BlockSpec((1, tc), lambda i, j: (0, j)),
        ],
        out_specs=pl.BlockSpec((tm, tc), lambda i, j: (i, j)),
        out_shape=jax.ShapeDtypeStruct((m, B_QK), BF16),
        compiler_params=_cparams(("parallel", "parallel")),
        name="causal_conv_silu",
    )(a1, a1, conv_w, conv_b.reshape(1, B_QK), kscale)


def _mlstm_kernel(q_ref, k_ref, v_ref, og_ref, z_ref, g_ref, gb_ref, hg_ref, o_ref, c_ref, n_ref, m_ref):
    ln = q_ref.shape[0]
    h = pl.program_id(1)

    @pl.when(pl.program_id(2) == 0)
    def _():
        c_ref[...] = jnp.zeros(c_ref.shape, F32)
        n_ref[...] = jnp.zeros(n_ref.shape, F32)
        m_ref[...] = jnp.zeros(m_ref.shape, F32)

    lane = lax.broadcasted_iota(jnp.int32, (ln, LANES), 1)
    gt = g_ref[...] + gb_ref[...]
    pre = jnp.where(lane < B_HEADS, gt, jnp.minimum(gt, 0.0) - jnp.log1p(jnp.exp(-jnp.abs(gt))))
    r = lax.broadcasted_iota(jnp.int32, (ln, ln), 0)
    c = lax.broadcasted_iota(jnp.int32, (ln, ln), 1)
    causal = c <= r
    cum = _dot(jnp.where(causal, 1.0, 0.0), pre, precision=HIGHEST)
    b_col = jnp.sum(jnp.where(lane == B_HEADS + h, cum, 0.0), axis=1, keepdims=True)
    i_col = jnp.sum(jnp.where(lane == h, pre, 0.0), axis=1, keepdims=True)
    sub = lax.broadcasted_iota(jnp.int32, (LANES, ln), 0)
    b_row = jnp.sum(jnp.where(sub == B_HEADS + h, cum.T, 0.0), axis=0, keepdims=True)
    i_row = jnp.sum(jnp.where(sub == h, pre.T, 0.0), axis=0, keepdims=True)
    g_tot = b_col[ln - 1:ln, :]
    m_st = m_ref[...]

    q = q_ref[...]
    k = k_ref[...]
    v = v_ref[...]
    log_d = jnp.where(causal, b_col - b_row + i_row, NEG)
    m_inter = b_col + m_st
    m_j = jnp.maximum(jnp.max(log_d, axis=1, keepdims=True), m_inter)
    smat = _dot_nt(q, k) * jnp.exp(log_d - m_j)
    w_inter = jnp.exp(m_inter - m_j)
    num = _dot(smat.astype(BF16), v) + w_inter * _dot(q, c_ref[...].astype(BF16))
    qn = jnp.sum(q.astype(F32) * n_ref[...], axis=1, keepdims=True)
    den = jnp.sum(smat, axis=1, keepdims=True) + w_inter * qn
    hh = num / jnp.maximum(jnp.abs(den), jnp.exp(-m_j))

    a_col = g_tot - b_col + i_col
    m_loc = jnp.max(a_col, axis=0, keepdims=True)
    kw = k.astype(F32) * jnp.exp(a_col - m_loc)
    c_loc = _dot(kw.T.astype(BF16), v)
    n_loc = jnp.sum(kw, axis=0, keepdims=True)
    m_new = jnp.maximum(g_tot + m_st, m_loc)
    s_old = jnp.exp(g_tot + m_st - m_new)
    s_new = jnp.exp(m_loc - m_new)
    c_ref[...] = s_old * c_ref[...] + s_new * c_loc
    n_ref[...] = s_old * n_ref[...] + s_new * n_loc
    m_ref[...] = m_new

    hn = hh * lax.rsqrt(jnp.mean(hh * hh, axis=-1, keepdims=True) + EPS) * hg_ref[...]
    o_ref[...] = (jax.nn.sigmoid(og_ref[...].astype(F32)) * hn * _silu(z_ref[...].astype(F32))).astype(o_ref.dtype)


def _mlstm(qk, a1, g1, gbias, head_g, bsz, seq):
    ln = ML_CHUNK
    nc = seq // ln
    vcol = B_QK // B_DV
    nv = B_WIDTH // B_DV
    return pl.pallas_call(
        _mlstm_kernel,
        grid=(bsz, B_HEADS, nc),
        in_specs=[
            pl.BlockSpec((ln, B_DK), lambda b, h, n: (b * nc + n, h)),
            pl.BlockSpec((ln, B_DK), lambda b, h, n: (b * nc + n, B_HEADS + h)),
            pl.BlockSpec((ln, B_DV), lambda b, h, n: (b * nc + n, vcol + h)),
            pl.BlockSpec((ln, B_DV), lambda b, h, n: (b * nc + n, vcol + nv + h)),
            pl.BlockSpec((ln, B_DV), lambda b, h, n: (b * nc + n, vcol + 2 * nv + h)),
            pl.BlockSpec((ln, LANES), lambda b, h, n: (b * nc + n, 0)),
            pl.BlockSpec((1, LANES), lambda b, h, n: (0, 0)),
            pl.BlockSpec((1, B_DV), lambda b, h, n: (0, h)),
        ],
        out_specs=pl.BlockSpec((ln, B_DV), lambda b, h, n: (b * nc + n, h)),
        out_shape=jax.ShapeDtypeStruct((bsz * seq, B_WIDTH), BF16),
        scratch_shapes=[
            pltpu.VMEM((B_DK, B_DV), F32),
            pltpu.VMEM((1, B_DK), F32),
            pltpu.VMEM((1, 1), F32),
        ],
        compiler_params=_cparams(("parallel", "parallel", "arbitrary")),
        name="mlstm_chunkwise",
    )(qk, qk, a1, a1, a1, g1, gbias, head_g.reshape(1, B_WIDTH))


def _outproj1_kernel(y_ref, x_ref, gate_ref, w_ref, fg_ref, o_ref):
    x2 = x_ref[...] + gate_ref[...] * _dot(y_ref[...], w_ref[...])
    o_ref[...] = x2 * lax.rsqrt(jnp.mean(x2 * x2, axis=-1, keepdims=True) + EPS) * fg_ref[...]


def _outproj1(y, x2d, gate, w, final_g, seq):
    m, d = x2d.shape
    tm = TM_OUT
    per_b = seq // tm
    row = lambda i: (i, 0)
    return pl.pallas_call(
        _outproj1_kernel,
        grid=(m // tm,),
        in_specs=[
            pl.BlockSpec((tm, y.shape[1]), row),
            pl.BlockSpec((tm, d), row),
            pl.BlockSpec((None, 1, d), lambda i: (i // per_b, 0, 0)),
            pl.BlockSpec(w.shape, lambda i: (0, 0)),
            pl.BlockSpec((1, d), lambda i: (0, 0)),
        ],
        out_specs=pl.BlockSpec((tm, d), row),
        out_shape=jax.ShapeDtypeStruct((m, d), F32),
        compiler_params=_cparams(("parallel",)),
        name="mlstm_out_proj_final_norm",
    )(y, x2d, gate[:, None, :], w, final_g.reshape(1, d))


def _alibi_slopes():
    return np.asarray(2.0 ** (-8.0 * np.arange(1, A_HEADS + 1) / A_HEADS), np.float32)


def _selection_overlap_t(seq, cols):
    n_cmp = seq // CMP_STRIDE - CMP_LEN // CMP_STRIDE + 1
    n_sel = seq // SLC_LEN
    c0 = np.arange(n_cmp)[None, :] * CMP_STRIDE
    s0 = np.arange(n_sel)[:, None] * SLC_LEN
    ov = np.clip(np.minimum(c0 + CMP_LEN, s0 + SLC_LEN) - np.maximum(c0, s0), 0, None) / CMP_LEN
    out = np.zeros((n_sel, cols), np.float32)
    out[:, :n_cmp] = ov
    return out


def _layer0_weights(w_in):
    d = w_in.shape[0]
    kv0 = A_WIDTH
    gl0 = kv0 + 6 * A_GROUPS * A_DH
    z0 = gl0 + 3 * A_HEADS
    wq = w_in[:, :A_WIDTH] * (A_DH ** -0.5 * LOG2E)
    wz = w_in[:, z0:z0 + A_WIDTH]
    wkv = w_in[:, kv0:gl0].reshape(d, 6, A_GROUPS, A_DH)
    branches = [wkv[:, 2 * r:2 * r + 2].transpose(0, 2, 1, 3).reshape(d, A_GROUPS * 2 * A_DH) for r in range(3)]
    w0 = jnp.concatenate([wq, wz] + branches, axis=1).astype(BF16)
    wgl = w_in[:, gl0:z0].reshape(d, 3, A_GROUPS, A_HG).transpose(0, 2, 1, 3).reshape(d, A_GROUPS, 3 * A_HG)
    wgl = jnp.pad(wgl, ((0, 0), (0, 0), (0, LANES - 3 * A_HG))).reshape(d, A_GROUPS * LANES).astype(BF16)
    return w0, wgl


def _compress_weights(pe, w1, b1, w2, b2):
    half = CMP_LEN // 2
    w1r = w1.reshape(2, 2, half, A_DH, CMP_HIDDEN)
    wc = jnp.einsum("khldn,kq->lkdqhn", w1r, jnp.eye(2, dtype=w1.dtype))
    wc = wc.reshape(half * 2 * A_DH, 4 * CMP_HIDDEN).astype(BF16)
    pec = pe.reshape(2, 2, half, A_DH).transpose(1, 2, 0, 3).reshape(2, half * 2 * A_DH)
    pec = jnp.pad(pec, ((0, SUBLANES - 2), (0, 0))).astype(BF16)
    b1c = b1.reshape(1, 2 * CMP_HIDDEN)
    w2k = jnp.zeros((CMP_HIDDEN, 2 * LANES), F32)
    w2k = w2k.at[:, 0:A_DH].set(w2[0]).at[:, LANES + A_DH:2 * LANES].set(w2[0])
    b2k = jnp.zeros((1, 2 * LANES), F32)
    b2k = b2k.at[0, 0:A_DH].set(b2[0]).at[0, LANES + A_DH:2 * LANES].set(b2[0])
    w2vt = w2[1].T.astype(BF16)
    b2v = jnp.broadcast_to(b2[1][:, None], (A_DH, LANES))
    return wc, pec, b1c, w2k.astype(BF16), b2k, w2vt, b2v


def _layer1_weights(w_in):
    g0 = B_QK + B_WIDTH
    w_main = jnp.concatenate([w_in[:, :g0], w_in[:, g0 + 2 * B_HEADS:]], axis=1).astype(BF16)
    wg = jnp.pad(w_in[:, g0:g0 + 2 * B_HEADS], ((0, 0), (0, LANES - 2 * B_HEADS))).astype(BF16)
    return w_main, wg


def kernel(x, c, ada_w, ada_b, norm_g, final_g, a_w_in, a_cmp_pe, a_cmp_w1, a_cmp_b1, a_cmp_w2, a_cmp_b2,
           a_w_out, b_w_in, b_conv_w, b_conv_b, b_gate_b, b_head_g, b_w_out):
    bsz, seq, d = x.shape
    assert d == D_MODEL and seq % (2 * TM_PROJ) == 0 and ada_w.shape[0] == 2
    m = bsz * seq
    x2d = x.reshape(m, d)
    mod = _ada_mod(c, ada_w, ada_b)

    w0, wgl = _layer0_weights(a_w_in[0])
    a0, gl = _normmod_matmul(x2d, norm_g[0], mod[0, :, d:2 * d], mod[0, :, :d], w0, wgl, seq, "nsa_in_proj")
    z_col = 1
    kv_col = 2 * A_WIDTH // LANES
    n_chunk = seq // CMP_STRIDE
    n_cmp = n_chunk - CMP_LEN // CMP_STRIDE + 1
    n_sel = seq // SLC_LEN
    n_top = min(SLC_TOP, n_sel)
    assert n_sel % SUBLANES == 0 and n_sel <= LANES
    kvsrc = a0[:, 2 * A_WIDTH:2 * A_WIDTH + A_GROUPS * LANES]
    xc = kvsrc.reshape(bsz, n_chunk, CMP_STRIDE, A_GROUPS, LANES).transpose(0, 3, 1, 2, 4)
    xc = xc.reshape(bsz, A_GROUPS, n_chunk, CMP_STRIDE * LANES)
    kc, vct = _compress(xc, *_compress_weights(a_cmp_pe[0], a_cmp_w1[0], a_cmp_b1[0], a_cmp_w2[0], a_cmp_b2[0]))
    slopes = jnp.asarray(np.broadcast_to(
        np.pad((_alibi_slopes() * np.float32(LOG2E)).reshape(A_GROUPS, A_HG), ((0, 0), (0, SUBLANES - A_HG)))[:, :, None],
        (A_GROUPS, SUBLANES, LANES)))
    ovt = jnp.asarray(_selection_overlap_t(seq, n_chunk))
    o_cmp, sb = _cmp_attn(a0, kc, vct, gl, slopes, ovt, bsz, seq, n_cmp, n_sel, n_top)
    o_slc = _slc_attn(a0, sb, gl, slopes, bsz, seq, kv_col + A_GROUPS)
    o_win = _win_attn(a0, gl, slopes, bsz, seq, kv_col + 2 * A_GROUPS)
    x1 = _outproj0(o_cmp, o_slc, o_win, a0, x2d, mod[0, :, 2 * d:], a_w_out[0].astype(BF16), seq, z_col)

    w1m, w1g = _layer1_weights(b_w_in[0])
    a1, g1 = _normmod_matmul(x1, norm_g[1], mod[1, :, d:2 * d], mod[1, :, :d], w1m, w1g, seq, "mlstm_in_proj")
    kscale = jnp.concatenate([jnp.ones((1, B_QK // 2), F32), jnp.full((1, B_QK // 2), B_DK ** -0.5, F32)], axis=1)
    qk = _conv_silu(a1, b_conv_w[0], b_conv_b[0], kscale, seq)
    gbias = jnp.pad(b_gate_b[0].reshape(1, 2 * B_HEADS), ((0, 0), (0, LANES - 2 * B_HEADS)))
    y1 = _mlstm(qk, a1, g1, gbias, b_head_g[0], bsz, seq)
    out = _outproj1(y1, x1, mod[1, :, 2 * d:], b_w_out[0].astype(BF16), final_g, seq)
    return out.reshape(bsz, seq, d)
```

```python
import functools

import numpy as np
import jax
import jax.numpy as jnp
from jax import lax
from jax.experimental import pallas as pl
from jax.experimental.pallas import tpu as pltpu

F32 = jnp.float32
BF16 = jnp.bfloat16
HIGHEST = lax.Precision.HIGHEST

EPS = 1e-6
NEG = -1e30
LOG2E = float(np.log2(np.e))

D_MODEL = 1024
A_HEADS = 16
A_GROUPS = 4
A_HG = A_HEADS // A_GROUPS
A_DH = 64
A_WIDTH = A_HEADS * A_DH
CMP_LEN = 32
CMP_STRIDE = 16
CMP_HIDDEN = 256
SLC_LEN = 64
SLC_SHIFT = 6
SLC_TOP = 16
WIN = 512
B_HEADS = 8
B_DK = 128
B_DV = 256
B_WIDTH = B_HEADS * B_DV
B_QK = 2 * B_HEADS * B_DK
CONV_W = 4

LANES = 128
SUBLANES = 8
VMEM_LIMIT = 56 * 1024 * 1024

TQ = 256
TQ_CMP = 1024
WIN_SUBTILES = 2
ML_CHUNK = 256
TM_PROJ = 1024
TN_PROJ0 = 1792
TN_PROJ1 = 2048
TM_OUT = 512
TM_CONV = 1024
TC_CONV = 512

assert WIN % TQ == 0 and 1 << SLC_SHIFT == SLC_LEN


def _dot(a, b, **kw):
    return jnp.dot(a, b, preferred_element_type=F32, **kw)


def _dot_nt(a, b):
    return lax.dot_general(a, b, (((1,), (1,)), ((), ())), preferred_element_type=F32)


def _cparams(sem):
    return pltpu.CompilerParams(dimension_semantics=sem, vmem_limit_bytes=VMEM_LIMIT)


def _silu(x):
    return x * jax.nn.sigmoid(x)


def _ada_kernel(c_ref, w_ref, b_ref, o_ref):
    o_ref[...] = _dot(c_ref[...], w_ref[...], precision=HIGHEST) + b_ref[...]


def _ada_mod(c, ada_w, ada_b):
    depth, d, d3 = ada_w.shape
    bsz = c.shape[0]
    nj = d3 // d
    return pl.pallas_call(
        _ada_kernel,
        grid=(depth, nj),
        in_specs=[
            pl.BlockSpec((bsz, d), lambda i, j: (0, 0)),
            pl.BlockSpec((None, d, d), lambda i, j: (i, 0, j)),
            pl.BlockSpec((None, 1, d), lambda i, j: (i, 0, j)),
        ],
        out_specs=pl.BlockSpec((None, bsz, d), lambda i, j: (i, 0, j)),
        out_shape=jax.ShapeDtypeStruct((depth, bsz, d3), F32),
        compiler_params=_cparams(("parallel", "parallel")),
        name="ada_mod",
    )(c, ada_w, ada_b.reshape(depth, 1, d3))


def _normmod_kernel(x_ref, g_ref, sc_ref, sh_ref, w_ref, wg_ref, o_ref, og_ref, h_ref):
    @pl.when(pl.program_id(1) == 0)
    def _():
        x = x_ref[...]
        xn = x * lax.rsqrt(jnp.mean(x * x, axis=-1, keepdims=True) + EPS)
        h = (xn * g_ref[...]) * (1.0 + sc_ref[...]) + sh_ref[...]
        hb = h.astype(BF16)
        h_ref[...] = hb
        half = hb.shape[0] // 2
        og_ref[0:half, :] = _dot(hb[0:half, :], wg_ref[...])
        og_ref[half:, :] = _dot(hb[half:, :], wg_ref[...])

    o_ref[...] = _dot(h_ref[...], w_ref[...]).astype(o_ref.dtype)


def _normmod_matmul(x2d, g, scale, shift, w, wg, seq, tn, name):
    m, d = x2d.shape
    n = w.shape[1]
    ng = wg.shape[1]
    tm = TM_PROJ
    assert m % tm == 0 and n % tn == 0 and seq % tm == 0
    per_b = seq // tm
    w_tiles = w.reshape(d, n // tn, tn).transpose(1, 0, 2)
    return pl.pallas_call(
        _normmod_kernel,
        grid=(m // tm, n // tn),
        in_specs=[
            pl.BlockSpec((tm, d), lambda i, j: (i, 0)),
            pl.BlockSpec((1, d), lambda i, j: (0, 0)),
            pl.BlockSpec((None, 1, d), lambda i, j: (i // per_b, 0, 0)),
            pl.BlockSpec((None, 1, d), lambda i, j: (i // per_b, 0, 0)),
            pl.BlockSpec((None, d, tn), lambda i, j: (j, 0, 0)),
            pl.BlockSpec((d, ng), lambda i, j: (0, 0)),
        ],
        out_specs=[
            pl.BlockSpec((tm, tn), lambda i, j: (i, j)),
            pl.BlockSpec((tm, ng), lambda i, j: (i, 0)),
        ],
        out_shape=[jax.ShapeDtypeStruct((m, n), BF16), jax.ShapeDtypeStruct((m, ng), F32)],
        scratch_shapes=[pltpu.VMEM((tm, d), BF16)],
        compiler_params=_cparams(("parallel", "arbitrary")),
        name=name,
    )(x2d, g.reshape(1, d), scale[:, None, :], shift[:, None, :], w_tiles, wg)


def _compress_kernel(x_ref, wc_ref, pe_ref, b1_ref, w2k_ref, b2k_ref, w2vt_ref, b2v_ref, ok_ref, ov_ref):
    a = _dot(x_ref[...], wc_ref[...])
    pb = _dot(pe_ref[...], wc_ref[...])
    n = a.shape[0]
    hid = CMP_HIDDEN
    hs = []
    for kv in range(2):
        c0 = kv * 2 * hid
        first = a[:, c0:c0 + hid]
        second = pltpu.roll(a[:, c0 + hid:c0 + 2 * hid], n - 1, 0)
        bias = pb[0:1, c0:c0 + hid] + pb[1:2, c0 + hid:c0 + 2 * hid] + b1_ref[:, kv * hid:(kv + 1) * hid]
        hs.append(jax.nn.gelu(first + second + bias))
    ok_ref[...] = (_dot(hs[0].astype(BF16), w2k_ref[...]) + b2k_ref[...]).astype(ok_ref.dtype)
    ov_ref[...] = (_dot_nt(w2vt_ref[...], hs[1].astype(BF16)) + b2v_ref[:, 0:1]).astype(ov_ref.dtype)


def _compress(xc, wc, pec, b1c, w2k, b2k, w2vt, b2v):
    bsz, g, nchunk, kdim = xc.shape
    full = lambda a: pl.BlockSpec(a.shape, lambda b, gi: (0,) * a.ndim)
    return pl.pallas_call(
        _compress_kernel,
        grid=(bsz, g),
        in_specs=[pl.BlockSpec((None, None, nchunk, kdim), lambda b, gi: (b, gi, 0, 0)),
                  full(wc), full(pec), full(b1c), full(w2k), full(b2k), full(w2vt), full(b2v)],
        out_specs=[
            pl.BlockSpec((None, None, nchunk, 2 * LANES), lambda b, gi: (b, gi, 0, 0)),
            pl.BlockSpec((None, None, A_DH, nchunk), lambda b, gi: (b, gi, 0, 0)),
        ],
        out_shape=[
            jax.ShapeDtypeStruct((bsz, g, nchunk, 2 * LANES), BF16),
            jax.ShapeDtypeStruct((bsz, g, A_DH, nchunk), BF16),
        ],
        compiler_params=_cparams(("parallel", "parallel")),
        name="compress_tokens",
    )(xc, wc, pec, b1c, w2k, b2k, w2vt, b2v)


def _cmp_attn_kernel(q_ref, k_ref, vt_ref, gl_ref, sl_ref, ovt_ref, o_ref, sb_ref, *, n_cmp, n_sel, n_top):
    tq = q_ref.shape[0]
    ncol = k_ref.shape[0]
    q0 = pl.program_id(2) * tq
    tpos = lax.broadcasted_iota(jnp.int32, (ncol, tq), 1) + q0
    cidx = lax.broadcasted_iota(jnp.int32, (ncol, tq), 0)
    dist = tpos - (cidx * CMP_STRIDE + (CMP_LEN - 1))
    valid = (dist >= 0) & (cidx < n_cmp)
    distf = dist.astype(F32)
    gate_t = jax.nn.sigmoid(gl_ref[...]).T
    vt = vt_ref[...]
    psum = jnp.zeros((ncol, tq), F32)
    scores = [_dot_nt(k_ref[:, (hg % 2) * LANES:(hg % 2 + 1) * LANES], q_ref[:, (hg // 2) * LANES:(hg // 2 + 1) * LANES])
              for hg in range(A_HG)]
    probs = []
    for hg in range(A_HG):
        s = jnp.where(valid, scores[hg] - sl_ref[hg:hg + 1, 0:1] * distf, NEG)
        m = jnp.max(s, axis=0, keepdims=True)
        m = jnp.where(m > 0.5 * NEG, m, 0.0)
        pr = jnp.exp2(s - m)
        pr = pr * (1.0 / jnp.maximum(jnp.sum(pr, axis=0, keepdims=True), 1e-30))
        psum = psum + pr
        probs.append(pr.astype(BF16))
    outs = [_dot(vt, probs[hg]) * gate_t[hg:hg + 1, :] for hg in range(A_HG)]
    o_ref[...] = jnp.concatenate(outs, axis=0).T.astype(o_ref.dtype)

    imp = _dot(ovt_ref[...], psum, precision=HIGHEST)
    blk = lax.broadcasted_iota(jnp.int32, (n_sel, tq), 0)
    blkf = blk.astype(F32)
    cur = lax.shift_right_logical(lax.broadcasted_iota(jnp.int32, (n_sel, tq), 1) + q0, SLC_SHIFT)
    forced = (blk == 0) | (blk == cur) | (blk == cur - 1)
    score = jnp.where(forced, 3e38, jnp.where(blk <= cur, imp, -1.0))
    sel = jnp.zeros((n_sel, tq), F32)
    for _ in range(n_top):
        mx = jnp.max(score, axis=0, keepdims=True)
        idx = jnp.min(jnp.where(score == mx, blkf, 1e9), axis=0, keepdims=True)
        pick = blkf == idx
        sel = jnp.where(pick, 1.0, sel)
        score = jnp.where(pick, -3.0, score)
    bias_t = jnp.where(sel > 0.0, 0.0, NEG)
    bias_t = jnp.concatenate([bias_t, jnp.zeros((LANES - n_sel, tq), F32)], axis=0)
    sb_ref[...] = bias_t.T.astype(sb_ref.dtype)


def _cmp_attn(a0, kc, vct, gl, slopes, ovt, bsz, seq, n_cmp, n_sel, n_top):
    tq = TQ_CMP
    nq = seq // tq
    ncol = kc.shape[2]
    kern = functools.partial(_cmp_attn_kernel, n_cmp=n_cmp, n_sel=n_sel, n_top=n_top)
    return pl.pallas_call(
        kern,
        grid=(bsz, A_GROUPS, nq),
        in_specs=[
            pl.BlockSpec((tq, 2 * LANES), lambda b, g, i: (b * nq + i, g)),
            pl.BlockSpec((None, None, ncol, 2 * LANES), lambda b, g, i: (b, g, 0, 0)),
            pl.BlockSpec((None, None, A_DH, ncol), lambda b, g, i: (b, g, 0, 0)),
            pl.BlockSpec((tq, LANES), lambda b, g, i: (b * nq + i, g)),
            pl.BlockSpec((None, SUBLANES, LANES), lambda b, g, i: (g, 0, 0)),
            pl.BlockSpec(ovt.shape, lambda b, g, i: (0, 0)),
        ],
        out_specs=[
            pl.BlockSpec((tq, 2 * LANES), lambda b, g, i: (b * nq + i, g)),
            pl.BlockSpec((None, None, tq, LANES), lambda b, g, i: (b, g, i, 0)),
        ],
        out_shape=[
            jax.ShapeDtypeStruct((bsz * seq, A_WIDTH), BF16),
            jax.ShapeDtypeStruct((bsz, A_GROUPS, seq, LANES), BF16),
        ],
        compiler_params=_cparams(("parallel", "parallel", "parallel")),
        name="cmp_attention_topk",
    )(a0, kc, vct, gl, slopes, ovt)


def _stage_kv(kv_ref, ka_ref, vt_ref, row0, chunk, aug_fn):
    seq = kv_ref.shape[0]
    for t in range(seq // chunk):
        kvf = kv_ref[t * chunk:(t + 1) * chunk, :].astype(F32)
        low = lax.broadcasted_iota(jnp.int32, kvf.shape, 1) < A_DH
        rolled = pltpu.roll(kvf, A_DH, 1)
        r0 = row0 + t * chunk
        ka_ref[0, r0:r0 + chunk, 0:LANES] = jnp.where(low, kvf, 0.0).astype(BF16)
        ka_ref[1, r0:r0 + chunk, 0:LANES] = jnp.where(low, 0.0, rolled).astype(BF16)
        aug = aug_fn(t)
        ka_ref[0, r0:r0 + chunk, LANES:2 * LANES] = aug
        ka_ref[1, r0:r0 + chunk, LANES:2 * LANES] = aug
        vt_ref[row0 // chunk + t] = kvf.T[A_DH:, :].astype(BF16)


def _slc_kernel(q_ref, sb_ref, kv_ref, gl_ref, sl_ref, o_ref,
                ka_ref, vt_ref, sd_ref, bd_ref, m_ref, l_ref, acc_ref, qa_ref, sa_ref, sb2_ref):
    tq = q_ref.shape[0]
    tk = tq
    i = pl.program_id(2)

    @pl.when(i == 0)
    def _stage():
        def onehot(t):
            blk = lax.shift_right_logical(lax.broadcasted_iota(jnp.int32, (tk, LANES), 0) + t * tk, SLC_SHIFT)
            return jnp.where(blk == lax.broadcasted_iota(jnp.int32, (tk, LANES), 1), 1.0, 0.0).astype(BF16)

        _stage_kv(kv_ref, ka_ref, vt_ref, 0, tk, onehot)
        kk = lax.broadcasted_iota(jnp.int32, (tk, tq), 0)
        qq = lax.broadcasted_iota(jnp.int32, (tk, tq), 1)
        d = (qq - kk).astype(F32)
        for hg in range(A_HG):
            sd = -sl_ref[hg:hg + 1, 0:1] * d
            sd_ref[hg] = sd
            bd_ref[hg] = jnp.where(kk <= qq, sd, NEG)

    sb = sb_ref[...]
    for p in range(2):
        qa_ref[p, :, 0:LANES] = q_ref[:, p * LANES:(p + 1) * LANES]
        qa_ref[p, :, LANES:2 * LANES] = sb
    m_ref[...] = jnp.full(m_ref.shape, NEG, F32)
    l_ref[...] = jnp.zeros(l_ref.shape, F32)
    acc_ref[...] = jnp.zeros(acc_ref.shape, F32)

    def scores(kt, s_ref):
        k0 = pl.multiple_of(kt * tk, tk)
        for hg in range(A_HG):
            s_ref[hg] = _dot_nt(ka_ref[hg % 2, pl.ds(k0, tk), :], qa_ref[hg // 2])

    def softmax_pv(kt, s_ref, bias_ref):
        off = ((i - kt) * tk).astype(F32)
        vt = vt_ref[kt]
        m_all = m_ref[...]
        l_all = l_ref[...]
        probs, alphas, m_rows, l_rows = [], [], [], []
        for hg in range(A_HG):
            cst = -sl_ref[hg:hg + 1, 0:1] * off
            s = s_ref[hg] + bias_ref[hg]
            m_prev = m_all[hg:hg + 1, :]
            m_new = jnp.maximum(m_prev, jnp.max(s, axis=0, keepdims=True) + cst)
            alpha = jnp.exp2(m_prev - m_new)
            pm = jnp.exp2(s - (m_new - cst))
            l_rows.append(alpha * l_all[hg:hg + 1, :] + jnp.sum(pm, axis=0, keepdims=True))
            m_rows.append(m_new)
            alphas.append(alpha)
            probs.append(pm.astype(BF16))
        pad = [jnp.zeros((SUBLANES - A_HG, tq), F32)]
        m_ref[...] = jnp.concatenate(m_rows + pad, axis=0)
        l_ref[...] = jnp.concatenate(l_rows + pad, axis=0)
        for hg in range(A_HG):
            acc_ref[hg] = acc_ref[hg] * alphas[hg] + _dot(vt, probs[hg])

    scores(0, sa_ref)

    def body(j, carry):
        kt = 2 * j
        scores(kt + 1, sb2_ref)
        softmax_pv(kt, sa_ref, sd_ref)
        scores(kt + 2, sa_ref)
        softmax_pv(kt + 1, sb2_ref, sd_ref)
        return carry

    lax.fori_loop(0, i // 2, body, 0)

    @pl.when(i % 2 == 0)
    def _():
        softmax_pv(i, sa_ref, bd_ref)

    @pl.when(i % 2 == 1)
    def _():
        scores(i, sb2_ref)
        softmax_pv(i - 1, sa_ref, sd_ref)
        softmax_pv(i, sb2_ref, bd_ref)

    gate_t = jax.nn.sigmoid(gl_ref[...]).T
    outs = []
    for hg in range(A_HG):
        w = gate_t[A_HG + hg:A_HG + hg + 1, :] / jnp.maximum(l_ref[hg:hg + 1, :], 1e-30)
        outs.append(acc_ref[hg] * w)
    o_ref[...] = jnp.concatenate(outs, axis=0).T.astype(o_ref.dtype)


def _slc_attn(a0, sb, gl, slopes, bsz, seq, kv_col):
    nq = seq // TQ
    return pl.pallas_call(
        _slc_kernel,
        grid=(bsz, A_GROUPS, nq),
        in_specs=[
            pl.BlockSpec((TQ, 2 * LANES), lambda b, g, i: (b * nq + i, g)),
            pl.BlockSpec((None, None, TQ, LANES), lambda b, g, i: (b, g, i, 0)),
            pl.BlockSpec((seq, LANES), lambda b, g, i: (b, kv_col + g)),
            pl.BlockSpec((TQ, LANES), lambda b, g, i: (b * nq + i, g)),
            pl.BlockSpec((None, SUBLANES, LANES), lambda b, g, i: (g, 0, 0)),
        ],
        out_specs=pl.BlockSpec((TQ, 2 * LANES), lambda b, g, i: (b * nq + i, g)),
        out_shape=jax.ShapeDtypeStruct((bsz * seq, A_WIDTH), BF16),
        scratch_shapes=[
            pltpu.VMEM((2, seq, 2 * LANES), BF16),
            pltpu.VMEM((seq // TQ, A_DH, TQ), BF16),
            pltpu.VMEM((A_HG, TQ, TQ), F32),
            pltpu.VMEM((A_HG, TQ, TQ), F32),
            pltpu.VMEM((SUBLANES, TQ), F32),
            pltpu.VMEM((SUBLANES, TQ), F32),
            pltpu.VMEM((A_HG, A_DH, TQ), F32),
            pltpu.VMEM((2, TQ, 2 * LANES), BF16),
            pltpu.VMEM((A_HG, TQ, TQ), F32),
            pltpu.VMEM((A_HG, TQ, TQ), F32),
        ],
        compiler_params=_cparams(("parallel", "parallel", "arbitrary")),
        name="selected_attention",
    )(a0, sb, a0, gl, slopes)


def _win_kernel(q_ref, kv_ref, gl_ref, sl_ref, o_ref, ka_ref, vt_ref, bw_ref):
    tq = TQ
    nsub = q_ref.shape[0] // tq
    wk = WIN + tq
    npad = WIN // tq
    i = pl.program_id(2)

    @pl.when(i == 0)
    def _stage():
        lane_p = lax.broadcasted_iota(jnp.int32, (WIN, LANES), 1)
        for e in range(2):
            ka_ref[e, 0:WIN, 0:LANES] = jnp.zeros((WIN, LANES), BF16)
            ka_ref[e, 0:WIN, LANES:2 * LANES] = jnp.where(lane_p == 0, 1.0, 0.0).astype(BF16)
        for t in range(npad):
            vt_ref[t] = jnp.zeros(vt_ref.shape[1:], BF16)
        _stage_kv(kv_ref, ka_ref, vt_ref, WIN, tq, lambda t: jnp.zeros((tq, LANES), BF16))
        kk = lax.broadcasted_iota(jnp.int32, (wk, tq), 0)
        qq = lax.broadcasted_iota(jnp.int32, (wk, tq), 1)
        dist = qq + WIN - kk
        ok = (dist >= 0) & (dist < WIN)
        distf = dist.astype(F32)
        for hg in range(A_HG):
            bw_ref[hg] = jnp.where(ok, -sl_ref[hg:hg + 1, 0:1] * distf, NEG)

    lane = lax.broadcasted_iota(jnp.int32, (tq, LANES), 1)
    negrow = jnp.where(lane == 0, NEG, 0.0).astype(BF16)
    gate_t = jax.nn.sigmoid(gl_ref[...]).T
    units = [(sub, hg) for sub in range(nsub) for hg in range(A_HG)]
    scores = []
    for sub in range(nsub):
        k0 = pl.multiple_of((i * nsub + sub) * tq, tq)
        rows = slice(sub * tq, (sub + 1) * tq)
        qa = [jnp.concatenate([q_ref[rows, p * LANES:(p + 1) * LANES], negrow], axis=1) for p in range(2)]
        scores += [_dot_nt(ka_ref[hg % 2, pl.ds(k0, wk), :], qa[hg // 2]) for hg in range(A_HG)]
    probs, scales = [], []
    for u, (sub, hg) in enumerate(units):
        s = scores[u] + bw_ref[hg]
        pm = jnp.exp2(s - jnp.max(s, axis=0, keepdims=True))
        l = jnp.sum(pm, axis=0, keepdims=True)
        gate = gate_t[2 * A_HG + hg:2 * A_HG + hg + 1, sub * tq:(sub + 1) * tq]
        scales.append(gate / jnp.maximum(l, 1e-30))
        probs.append(pm.astype(BF16))
    outs = []
    for u, (sub, hg) in enumerate(units):
        t0 = i * nsub + sub
        o = _dot(vt_ref[t0], probs[u][0:tq, :])
        for j in range(1, wk // tq):
            o = o + _dot(vt_ref[t0 + j], probs[u][j * tq:(j + 1) * tq, :])
        outs.append(o * scales[u])
    for sub in range(nsub):
        tile = jnp.concatenate(outs[sub * A_HG:(sub + 1) * A_HG], axis=0).T
        o_ref[sub * tq:(sub + 1) * tq, :] = tile.astype(o_ref.dtype)


def _win_attn(a0, gl, slopes, bsz, seq, kv_col):
    tb = WIN_SUBTILES * TQ
    nq = seq // tb
    return pl.pallas_call(
        _win_kernel,
        grid=(bsz, A_GROUPS, nq),
        in_specs=[
            pl.BlockSpec((tb, 2 * LANES), lambda b, g, i: (b * nq + i, g)),
            pl.BlockSpec((seq, LANES), lambda b, g, i: (b, kv_col + g)),
            pl.BlockSpec((tb, LANES), lambda b, g, i: (b * nq + i, g)),
            pl.BlockSpec((None, SUBLANES, LANES), lambda b, g, i: (g, 0, 0)),
        ],
        out_specs=pl.BlockSpec((tb, 2 * LANES), lambda b, g, i: (b * nq + i, g)),
        out_shape=jax.ShapeDtypeStruct((bsz * seq, A_WIDTH), BF16),
        scratch_shapes=[
            pltpu.VMEM((2, WIN + seq, 2 * LANES), BF16),
            pltpu.VMEM(((WIN + seq) // TQ, A_DH, TQ), BF16),
            pltpu.VMEM((A_HG, WIN + TQ, TQ), F32),
        ],
        compiler_params=_cparams(("parallel", "parallel", "arbitrary")),
        name="window_attention",
    )(a0, a0, gl, slopes)


def _outproj0_kernel(oc_ref, os_ref, ow_ref, z_ref, x_ref, gate_ref, w_ref, o_ref):
    o = oc_ref[...].astype(F32) + os_ref[...].astype(F32) + ow_ref[...].astype(F32)
    y = (o * _silu(z_ref[...].astype(F32))).astype(BF16)
    o_ref[...] = x_ref[...] + gate_ref[...] * _dot(y, w_ref[...])


def _outproj0(oc, osl, ow, a0, x2d, gate, w, seq, z_col):
    m, d = x2d.shape
    tm = TM_OUT
    per_b = seq // tm
    row = lambda i: (i, 0)
    return pl.pallas_call(
        _outproj0_kernel,
        grid=(m // tm,),
        in_specs=[
            pl.BlockSpec((tm, A_WIDTH), row),
            pl.BlockSpec((tm, A_WIDTH), row),
            pl.BlockSpec((tm, A_WIDTH), row),
            pl.BlockSpec((tm, A_WIDTH), lambda i: (i, z_col)),
            pl.BlockSpec((tm, d), row),
            pl.BlockSpec((None, 1, d), lambda i: (i // per_b, 0, 0)),
            pl.BlockSpec(w.shape, lambda i: (0, 0)),
        ],
        out_specs=pl.BlockSpec((tm, d), row),
        out_shape=jax.ShapeDtypeStruct((m, d), F32),
        compiler_params=_cparams(("parallel",)),
        name="nsa_out_proj",
    )(oc, osl, ow, a0, x2d, gate[:, None, :], w)


def _conv_kernel(x_ref, halo_ref, w_ref, b_ref, ks_ref, o_ref, *, per_b):
    first = (pl.program_id(0) % per_b) == 0
    halo = jnp.where(first, 0.0, halo_ref[...].astype(F32))
    xe = jnp.concatenate([halo, x_ref[...].astype(F32)], axis=0)
    y = xe[SUBLANES:, :] * w_ref[CONV_W - 1:CONV_W, :] + b_ref[...]
    for s in range(1, CONV_W):
        y = y + pltpu.roll(xe, s, 0)[SUBLANES:, :] * w_ref[CONV_W - 1 - s:CONV_W - s, :]
    o_ref[...] = (_silu(y) * ks_ref[...]).astype(o_ref.dtype)


def _conv_silu(a1, conv_w, conv_b, kscale, seq):
    m = a1.shape[0]
    tm, tc = TM_CONV, TC_CONV
    per_b = seq // tm
    hb = tm // SUBLANES
    kern = functools.partial(_conv_kernel, per_b=per_b)
    return pl.pallas_call(
        kern,
        grid=(m // tm, B_QK // tc),
        in_specs=[
            pl.BlockSpec((tm, tc), lambda i, j: (i, j)),
            pl.BlockSpec((SUBLANES, tc), lambda i, j: (jnp.maximum(i * hb - 1, 0), j)),
            pl.BlockSpec((CONV_W, tc), lambda i, j: (0, j)),
            pl.BlockSpec((1, tc), lambda i, j: (0, j)),
            pl.BlockSpec((1, tc), lambda i, j: (0, j)),
        ],
        out_specs=pl.BlockSpec((tm, tc), lambda i, j: (i, j)),
        out_shape=jax.ShapeDtypeStruct((m, B_QK), BF16),
        compiler_params=_cparams(("parallel", "parallel")),
        name="causal_conv_silu",
    )(a1, a1, conv_w, conv_b.reshape(1, B_QK), kscale)


def _mlstm_kernel(q_ref, k_ref, v_ref, g_ref, gb_ref, hg_ref, o_ref, c_ref, n_ref, m_ref):
    ln = q_ref.shape[0]
    nh = B_HEADS

    @pl.when(pl.program_id(1) == 0)
    def _():
        c_ref[...] = jnp.zeros(c_ref.shape, F32)
        n_ref[...] = jnp.zeros(n_ref.shape, F32)
        m_ref[...] = jnp.zeros(m_ref.shape, F32)

    lane = lax.broadcasted_iota(jnp.int32, (ln, LANES), 1)
    gt = g_ref[...] + gb_ref[...]
    pre = jnp.where(lane < nh, gt, jnp.minimum(gt, 0.0) - jnp.log1p(jnp.exp(-jnp.abs(gt)))) * LOG2E
    r = lax.broadcasted_iota(jnp.int32, (ln, ln), 0)
    c = lax.broadcasted_iota(jnp.int32, (ln, ln), 1)
    causal = c <= r
    cum = _dot(jnp.where(causal, 1.0, 0.0), pre, precision=HIGHEST)
    cum_t = cum.T
    pre_t = pre.T
    ones = jnp.ones((ln, LANES), BF16)

    def head(ref, h, width):
        return ref[:, h * width:(h + 1) * width]

    sqk = [_dot_nt(head(q_ref, h, B_DK), head(k_ref, h, B_DK)) for h in range(nh)]
    qc = [_dot(head(q_ref, h, B_DK), c_ref[h].astype(BF16)) for h in range(nh)]
    qn = [_dot(head(q_ref, h, B_DK), n_ref[h].astype(BF16))[:, 0:1] for h in range(nh)]

    smats, kwts, w_inters, m_js, carries = [], [], [], [], []
    for h in range(nh):
        b_col = cum[:, nh + h:nh + h + 1]
        i_col = pre[:, h:h + 1]
        b_row = cum_t[nh + h:nh + h + 1, :]
        i_row = pre_t[h:h + 1, :]
        g_tot = b_col[ln - 1:ln, :]
        m_st = m_ref[h][:, 0:1]
        log_d = jnp.where(causal, b_col + (i_row - b_row), NEG)
        m_inter = b_col + m_st
        m_j = jnp.maximum(jnp.max(log_d, axis=1, keepdims=True), m_inter)
        smats.append((sqk[h] * jnp.exp2(log_d - m_j)).astype(BF16))
        w_inters.append(jnp.exp2(m_inter - m_j))
        m_js.append(m_j)
        a_col = g_tot - b_col + i_col
        m_loc = jnp.max(a_col, axis=0, keepdims=True)
        kw = head(k_ref, h, B_DK).astype(F32) * jnp.exp2(a_col - m_loc)
        kwts.append(kw.T.astype(BF16))
        m_new = jnp.maximum(g_tot + m_st, m_loc)
        carries.append((jnp.exp2(g_tot + m_st - m_new), jnp.exp2(m_loc - m_new), m_new))

    intra = [_dot(smats[h], head(v_ref, h, B_DV)) for h in range(nh)]
    rowsum = [_dot(smats[h], ones)[:, 0:1] for h in range(nh)]
    c_loc = [_dot(kwts[h], head(v_ref, h, B_DV)) for h in range(nh)]
    n_loc = [_dot(kwts[h], ones) for h in range(nh)]

    for h in range(nh):
        s_old, s_new, m_new = carries[h]
        c_ref[h] = s_old * c_ref[h] + s_new * c_loc[h]
        n_ref[h] = s_old * n_ref[h] + s_new * n_loc[h]
        m_ref[h] = jnp.broadcast_to(m_new, (1, LANES))
        den = rowsum[h] + w_inters[h] * qn[h]
        scale = 1.0 / jnp.maximum(jnp.abs(den), jnp.exp2(-m_js[h]))
        u = intra[h] + w_inters[h] * qc[h]
        f = scale * lax.rsqrt(scale * scale * jnp.mean(u * u, axis=-1, keepdims=True) + EPS)
        o_ref[:, h * B_DV:(h + 1) * B_DV] = (u * f * head(hg_ref, h, B_DV)).astype(o_ref.dtype)


def _mlstm(qk, a1, g1, gbias, head_g, bsz, seq):
    ln = ML_CHUNK
    nc = seq // ln
    wq = B_QK // 2
    return pl.pallas_call(
        _mlstm_kernel,
        grid=(bsz, nc),
        in_specs=[
            pl.BlockSpec((ln, wq), lambda b, n: (b * nc + n, 0)),
            pl.BlockSpec((ln, wq), lambda b, n: (b * nc + n, 1)),
            pl.BlockSpec((ln, B_WIDTH), lambda b, n: (b * nc + n, B_QK // B_WIDTH)),
            pl.BlockSpec((ln, LANES), lambda b, n: (b * nc + n, 0)),
            pl.BlockSpec((1, LANES), lambda b, n: (0, 0)),
            pl.BlockSpec((1, B_WIDTH), lambda b, n: (0, 0)),
        ],
        out_specs=pl.BlockSpec((ln, B_WIDTH), lambda b, n: (b * nc + n, 0)),
        out_shape=jax.ShapeDtypeStruct((bsz * seq, B_WIDTH), BF16),
        scratch_shapes=[
            pltpu.VMEM((B_HEADS, B_DK, B_DV), F32),
            pltpu.VMEM((B_HEADS, B_DK, LANES), F32),
            pltpu.VMEM((B_HEADS, 1, LANES), F32),
        ],
        compiler_params=_cparams(("parallel", "arbitrary")),
        name="mlstm_chunkwise",
    )(qk, qk, a1, g1, gbias, head_g.reshape(1, B_WIDTH))


def _outproj1_kernel(h_ref, og_ref, z_ref, x_ref, gate_ref, w_ref, fg_ref, o_ref):
    y = jax.nn.sigmoid(og_ref[...].astype(F32)) * h_ref[...].astype(F32) * _silu(z_ref[...].astype(F32))
    x2 = x_ref[...] + gate_ref[...] * _dot(y.astype(BF16), w_ref[...])
    o_ref[...] = x2 * lax.rsqrt(jnp.mean(x2 * x2, axis=-1, keepdims=True) + EPS) * fg_ref[...]


def _outproj1(hn, a1, x2d, gate, w, final_g, seq):
    m, d = x2d.shape
    tm = TM_OUT
    per_b = seq // tm
    row = lambda i: (i, 0)
    og_col = B_QK // B_WIDTH + 1
    return pl.pallas_call(
        _outproj1_kernel,
        grid=(m // tm,),
        in_specs=[
            pl.BlockSpec((tm, B_WIDTH), row),
            pl.BlockSpec((tm, B_WIDTH), lambda i: (i, og_col)),
            pl.BlockSpec((tm, B_WIDTH), lambda i: (i, og_col + 1)),
            pl.BlockSpec((tm, d), row),
            pl.BlockSpec((None, 1, d), lambda i: (i // per_b, 0, 0)),
            pl.BlockSpec(w.shape, lambda i: (0, 0)),
            pl.BlockSpec((1, d), lambda i: (0, 0)),
        ],
        out_specs=pl.BlockSpec((tm, d), row),
        out_shape=jax.ShapeDtypeStruct((m, d), F32),
        compiler_params=_cparams(("parallel",)),
        name="mlstm_out_proj_final_norm",
    )(hn, a1, a1, x2d, gate[:, None, :], w, final_g.reshape(1, d))


def _alibi_slopes():
    return np.asarray(2.0 ** (-8.0 * np.arange(1, A_HEADS + 1) / A_HEADS), np.float32)


def _selection_overlap_t(seq, cols):
    n_cmp = seq // CMP_STRIDE - CMP_LEN // CMP_STRIDE + 1
    n_sel = seq // SLC_LEN
    c0 = np.arange(n_cmp)[None, :] * CMP_STRIDE
    s0 = np.arange(n_sel)[:, None] * SLC_LEN
    ov = np.clip(np.minimum(c0 + CMP_LEN, s0 + SLC_LEN) - np.maximum(c0, s0), 0, None) / CMP_LEN
    out = np.zeros((n_sel, cols), np.float32)
    out[:, :n_cmp] = ov
    return out


def _layer0_weights(w_in):
    d = w_in.shape[0]
    kv0 = A_WIDTH
    gl0 = kv0 + 6 * A_GROUPS * A_DH
    z0 = gl0 + 3 * A_HEADS
    wq = w_in[:, :A_WIDTH] * (A_DH ** -0.5 * LOG2E)
    wz = w_in[:, z0:z0 + A_WIDTH]
    wkv = w_in[:, kv0:gl0].reshape(d, 6, A_GROUPS, A_DH)
    branches = [wkv[:, 2 * r:2 * r + 2].transpose(0, 2, 1, 3).reshape(d, A_GROUPS * 2 * A_DH) for r in range(3)]
    w0 = jnp.concatenate([wq, wz] + branches, axis=1).astype(BF16)
    wgl = w_in[:, gl0:z0].reshape(d, 3, A_GROUPS, A_HG).transpose(0, 2, 1, 3).reshape(d, A_GROUPS, 3 * A_HG)
    wgl = jnp.pad(wgl, ((0, 0), (0, 0), (0, LANES - 3 * A_HG))).reshape(d, A_GROUPS * LANES).astype(BF16)
    return w0, wgl


def _compress_weights(pe, w1, b1, w2, b2):
    half = CMP_LEN // 2
    w1r = w1.reshape(2, 2, half, A_DH, CMP_HIDDEN)
    wc = jnp.einsum("khldn,kq->lkdqhn", w1r, jnp.eye(2, dtype=w1.dtype))
    wc = wc.reshape(half * 2 * A_DH, 4 * CMP_HIDDEN).astype(BF16)
    pec = pe.reshape(2, 2, half, A_DH).transpose(1, 2, 0, 3).reshape(2, half * 2 * A_DH)
    pec = jnp.pad(pec, ((0, SUBLANES - 2), (0, 0))).astype(BF16)
    b1c = b1.reshape(1, 2 * CMP_HIDDEN)
    w2k = jnp.zeros((CMP_HIDDEN, 2 * LANES), F32)
    w2k = w2k.at[:, 0:A_DH].set(w2[0]).at[:, LANES + A_DH:2 * LANES].set(w2[0])
    b2k = jnp.zeros((1, 2 * LANES), F32)
    b2k = b2k.at[0, 0:A_DH].set(b2[0]).at[0, LANES + A_DH:2 * LANES].set(b2[0])
    w2vt = w2[1].T.astype(BF16)
    b2v = jnp.broadcast_to(b2[1][:, None], (A_DH, LANES))
    return wc, pec, b1c, w2k.astype(BF16), b2k, w2vt, b2v


def _layer1_weights(w_in):
    g0 = B_QK + B_WIDTH
    w_main = jnp.concatenate([w_in[:, :g0], w_in[:, g0 + 2 * B_HEADS:]], axis=1).astype(BF16)
    wg = jnp.pad(w_in[:, g0:g0 + 2 * B_HEADS], ((0, 0), (0, LANES - 2 * B_HEADS))).astype(BF16)
    return w_main, wg


def kernel(x, c, ada_w, ada_b, norm_g, final_g, a_w_in, a_cmp_pe, a_cmp_w1, a_cmp_b1, a_cmp_w2, a_cmp_b2,
           a_w_out, b_w_in, b_conv_w, b_conv_b, b_gate_b, b_head_g, b_w_out):
    bsz, seq, d = x.shape
    assert d == D_MODEL and seq % (2 * TM_PROJ) == 0 and ada_w.shape[0] == 2
    m = bsz * seq
    x2d = x.reshape(m, d)
    mod = _ada_mod(c, ada_w, ada_b)

    w0, wgl = _layer0_weights(a_w_in[0])
    a0, gl = _normmod_matmul(x2d, norm_g[0], mod[0, :, d:2 * d], mod[0, :, :d], w0, wgl, seq, TN_PROJ0, "nsa_in_proj")
    z_col = 1
    kv_col = 2 * A_WIDTH // LANES
    n_chunk = seq // CMP_STRIDE
    n_cmp = n_chunk - CMP_LEN // CMP_STRIDE + 1
    n_sel = seq // SLC_LEN
    n_top = min(SLC_TOP, n_sel)
    assert n_sel % SUBLANES == 0 and n_sel <= LANES
    kvsrc = a0[:, 2 * A_WIDTH:2 * A_WIDTH + A_GROUPS * LANES]
    xc = kvsrc.reshape(bsz, n_chunk, CMP_STRIDE, A_GROUPS, LANES).transpose(0, 3, 1, 2, 4)
    xc = xc.reshape(bsz, A_GROUPS, n_chunk, CMP_STRIDE * LANES)
    kc, vct = _compress(xc, *_compress_weights(a_cmp_pe[0], a_cmp_w1[0], a_cmp_b1[0], a_cmp_w2[0], a_cmp_b2[0]))
    slopes = jnp.asarray(np.broadcast_to(
        np.pad((_alibi_slopes() * np.float32(LOG2E)).reshape(A_GROUPS, A_HG), ((0, 0), (0, SUBLANES - A_HG)))[:, :, None],
        (A_GROUPS, SUBLANES, LANES)))
    ovt = jnp.asarray(_selection_overlap_t(seq, n_chunk))
    o_cmp, sb = _cmp_attn(a0, kc, vct, gl, slopes, ovt, bsz, seq, n_cmp, n_sel, n_top)
    o_slc = _slc_attn(a0, sb, gl, slopes, bsz, seq, kv_col + A_GROUPS)
    o_win = _win_attn(a0, gl, slopes, bsz, seq, kv_col + 2 * A_GROUPS)
    x1 = _outproj0(o_cmp, o_slc, o_win, a0, x2d, mod[0, :, 2 * d:], a_w_out[0].astype(BF16), seq, z_col)

    w1m, w1g = _layer1_weights(b_w_in[0])
    a1, g1 = _normmod_matmul(x1, norm_g[1], mod[1, :, d:2 * d], mod[1, :, :d], w1m, w1g, seq, TN_PROJ1, "mlstm_in_proj")
    kscale = jnp.concatenate([jnp.ones((1, B_QK // 2), F32), jnp.full((1, B_QK // 2), B_DK ** -0.5, F32)], axis=1)
    qk = _conv_silu(a1, b_conv_w[0], b_conv_b[0], kscale, seq)
    gbias = jnp.pad(b_gate_b[0].reshape(1, 2 * B_HEADS), ((0, 0), (0, LANES - 2 * B_HEADS)))
    y1 = _mlstm(qk, a1, g1, gbias, b_head_g[0], bsz, seq)
    out = _outproj1(y1, a1, x1, mod[1, :, 2 * d:], b_w_out[0].astype(BF16), final_g, seq)
    return out.reshape(bsz, seq, d)
```

```python
import functools

import numpy as np
import jax
import jax.numpy as jnp
from jax import lax
from jax.experimental import pallas as pl
from jax.experimental.pallas import tpu as pltpu

F32 = jnp.float32
BF16 = jnp.bfloat16
HIGHEST = lax.Precision.HIGHEST

EPS = 1e-6
NEG = -1e30
LOG2E = float(np.log2(np.e))

D_MODEL = 1024
A_HEADS = 16
A_GROUPS = 4
A_HG = A_HEADS // A_GROUPS
A_DH = 64
A_WIDTH = A_HEADS * A_DH
CMP_LEN = 32
CMP_STRIDE = 16
CMP_HIDDEN = 256
SLC_LEN = 64
SLC_SHIFT = 6
SLC_TOP = 16
WIN = 512
B_HEADS = 8
B_DK = 128
B_DV = 256
B_WIDTH = B_HEADS * B_DV
B_QK = 2 * B_HEADS * B_DK
CONV_W = 4

LANES = 128
SUBLANES = 8
VMEM_LIMIT = 56 * 1024 * 1024

TQ = 256
TQ_CMP = 1024
WIN_SUBTILES = 2
ML_CHUNK = 256
TM_PROJ = 1024
TN_PROJ0 = 1792
TN_PROJ1 = 2048
TM_OUT = 512
TM_CONV = 1024
TC_CONV = 512

assert WIN % TQ == 0 and 1 << SLC_SHIFT == SLC_LEN


def _dot(a, b, **kw):
    return jnp.dot(a, b, preferred_element_type=F32, **kw)


def _dot_nt(a, b):
    return lax.dot_general(a, b, (((1,), (1,)), ((), ())), preferred_element_type=F32)


def _cparams(sem):
    return pltpu.CompilerParams(dimension_semantics=sem, vmem_limit_bytes=VMEM_LIMIT)


def _silu(x):
    return x * jax.nn.sigmoid(x)


def _ada_kernel(c_ref, w_ref, b_ref, o_ref):
    o_ref[...] = _dot(c_ref[...], w_ref[...], precision=HIGHEST) + b_ref[...]


def _ada_mod(c, ada_w, ada_b):
    depth, d, d3 = ada_w.shape
    bsz = c.shape[0]
    nj = d3 // d
    return pl.pallas_call(
        _ada_kernel,
        grid=(depth, nj),
        in_specs=[
            pl.BlockSpec((bsz, d), lambda i, j: (0, 0)),
            pl.BlockSpec((None, d, d), lambda i, j: (i, 0, j)),
            pl.BlockSpec((None, 1, d), lambda i, j: (i, 0, j)),
        ],
        out_specs=pl.BlockSpec((None, bsz, d), lambda i, j: (i, 0, j)),
        out_shape=jax.ShapeDtypeStruct((depth, bsz, d3), F32),
        compiler_params=_cparams(("parallel", "parallel")),
        name="ada_mod",
    )(c, ada_w, ada_b.reshape(depth, 1, d3))


def _normmod_kernel(x_ref, g_ref, sc_ref, sh_ref, w_ref, wg_ref, o_ref, og_ref, h_ref):
    @pl.when(pl.program_id(1) == 0)
    def _():
        x = x_ref[...]
        xn = x * lax.rsqrt(jnp.mean(x * x, axis=-1, keepdims=True) + EPS)
        h = (xn * g_ref[...]) * (1.0 + sc_ref[...]) + sh_ref[...]
        hb = h.astype(BF16)
        h_ref[...] = hb
        half = hb.shape[0] // 2
        og_ref[0:half, :] = _dot(hb[0:half, :], wg_ref[...])
        og_ref[half:, :] = _dot(hb[half:, :], wg_ref[...])

    o_ref[...] = _dot(h_ref[...], w_ref[...]).astype(o_ref.dtype)


def _normmod_matmul(x2d, g, scale, shift, w, wg, seq, tn, name):
    m, d = x2d.shape
    n = w.shape[1]
    ng = wg.shape[1]
    tm = TM_PROJ
    assert m % tm == 0 and n % tn == 0 and seq % tm == 0
    per_b = seq // tm
    w_tiles = w.reshape(d, n // tn, tn).transpose(1, 0, 2)
    return pl.pallas_call(
        _normmod_kernel,
        grid=(m // tm, n // tn),
        in_specs=[
            pl.BlockSpec((tm, d), lambda i, j: (i, 0)),
            pl.BlockSpec((1, d), lambda i, j: (0, 0)),
            pl.BlockSpec((None, 1, d), lambda i, j: (i // per_b, 0, 0)),
            pl.BlockSpec((None, 1, d), lambda i, j: (i // per_b, 0, 0)),
            pl.BlockSpec((None, d, tn), lambda i, j: (j, 0, 0)),
            pl.BlockSpec((d, ng), lambda i, j: (0, 0)),
        ],
        out_specs=[
            pl.BlockSpec((tm, tn), lambda i, j: (i, j)),
            pl.BlockSpec((tm, ng), lambda i, j: (i, 0)),
        ],
        out_shape=[jax.ShapeDtypeStruct((m, n), BF16), jax.ShapeDtypeStruct((m, ng), F32)],
        scratch_shapes=[pltpu.VMEM((tm, d), BF16)],
        compiler_params=_cparams(("parallel", "arbitrary")),
        name=name,
    )(x2d, g.reshape(1, d), scale[:, None, :], shift[:, None, :], w_tiles, wg)


def _compress_kernel(x_ref, wc_ref, pe_ref, b1_ref, w2k_ref, b2k_ref, w2vt_ref, b2v_ref, ok_ref, ov_ref):
    a = _dot(x_ref[...], wc_ref[...])
    pb = _dot(pe_ref[...], wc_ref[...])
    n = a.shape[0]
    hid = CMP_HIDDEN
    hs = []
    for kv in range(2):
        c0 = kv * 2 * hid
        first = a[:, c0:c0 + hid]
        second = pltpu.roll(a[:, c0 + hid:c0 + 2 * hid], n - 1, 0)
        bias = pb[0:1, c0:c0 + hid] + pb[1:2, c0 + hid:c0 + 2 * hid] + b1_ref[:, kv * hid:(kv + 1) * hid]
        hs.append(jax.nn.gelu(first + second + bias))
    ok_ref[...] = (_dot(hs[0].astype(BF16), w2k_ref[...]) + b2k_ref[...]).astype(ok_ref.dtype)
    ov_ref[...] = (_dot_nt(w2vt_ref[...], hs[1].astype(BF16)) + b2v_ref[:, 0:1]).astype(ov_ref.dtype)


def _compress(xc, wc, pec, b1c, w2k, b2k, w2vt, b2v):
    bsz, g, nchunk, kdim = xc.shape
    full = lambda a: pl.BlockSpec(a.shape, lambda b, gi: (0,) * a.ndim)
    return pl.pallas_call(
        _compress_kernel,
        grid=(bsz, g),
        in_specs=[pl.BlockSpec((None, None, nchunk, kdim), lambda b, gi: (b, gi, 0, 0)),
                  full(wc), full(pec), full(b1c), full(w2k), full(b2k), full(w2vt), full(b2v)],
        out_specs=[
            pl.BlockSpec((None, None, nchunk, 2 * LANES), lambda b, gi: (b, gi, 0, 0)),
            pl.BlockSpec((None, None, A_DH, nchunk), lambda b, gi: (b, gi, 0, 0)),
        ],
        out_shape=[
            jax.ShapeDtypeStruct((bsz, g, nchunk, 2 * LANES), BF16),
            jax.ShapeDtypeStruct((bsz, g, A_DH, nchunk), BF16),
        ],
        compiler_params=_cparams(("parallel", "parallel")),
        name="compress_tokens",
    )(xc, wc, pec, b1c, w2k, b2k, w2vt, b2v)


def _cmp_attn_kernel(q_ref, k_ref, vt_ref, gl_ref, sl_ref, ovt_ref, o_ref, sb_ref, *, n_cmp, n_sel, n_top):
    tq = q_ref.shape[0]
    ncol = k_ref.shape[0]
    q0 = pl.program_id(2) * tq
    tpos = lax.broadcasted_iota(jnp.int32, (ncol, tq), 1) + q0
    cidx = lax.broadcasted_iota(jnp.int32, (ncol, tq), 0)
    dist = tpos - (cidx * CMP_STRIDE + (CMP_LEN - 1))
    valid = (dist >= 0) & (cidx < n_cmp)
    distf = dist.astype(F32)
    gate_t = jax.nn.sigmoid(gl_ref[...]).T
    vt = vt_ref[...]
    psum = jnp.zeros((ncol, tq), F32)
    scores = [_dot_nt(k_ref[:, (hg % 2) * LANES:(hg % 2 + 1) * LANES], q_ref[:, (hg // 2) * LANES:(hg // 2 + 1) * LANES])
              for hg in range(A_HG)]
    probs = []
    for hg in range(A_HG):
        s = jnp.where(valid, scores[hg] - sl_ref[hg:hg + 1, 0:1] * distf, NEG)
        m = jnp.max(s, axis=0, keepdims=True)
        m = jnp.where(m > 0.5 * NEG, m, 0.0)
        pr = jnp.exp2(s - m)
        pr = pr * (1.0 / jnp.maximum(jnp.sum(pr, axis=0, keepdims=True), 1e-30))
        psum = psum + pr
        probs.append(pr.astype(BF16))
    outs = [_dot(vt, probs[hg]) * gate_t[hg:hg + 1, :] for hg in range(A_HG)]
    o_ref[...] = jnp.concatenate(outs, axis=0).T.astype(o_ref.dtype)

    imp = _dot(ovt_ref[...], psum, precision=HIGHEST)
    blk = lax.broadcasted_iota(jnp.int32, (n_sel, tq), 0)
    blkf = blk.astype(F32)
    cur = lax.shift_right_logical(lax.broadcasted_iota(jnp.int32, (n_sel, tq), 1) + q0, SLC_SHIFT)
    forced = (blk == 0) | (blk == cur) | (blk == cur - 1)
    score = jnp.where(forced, 3e38, jnp.where(blk <= cur, imp, -1.0))
    sel = jnp.zeros((n_sel, tq), F32)
    for _ in range(n_top):
        mx = jnp.max(score, axis=0, keepdims=True)
        idx = jnp.min(jnp.where(score == mx, blkf, 1e9), axis=0, keepdims=True)
        pick = blkf == idx
        sel = jnp.where(pick, 1.0, sel)
        score = jnp.where(pick, -3.0, score)
    bias_t = jnp.where(sel > 0.0, 0.0, NEG)
    bias_t = jnp.concatenate([bias_t, jnp.zeros((LANES - n_sel, tq), F32)], axis=0)
    sb_ref[...] = bias_t.T.astype(sb_ref.dtype)


def _cmp_attn(a0, kc, vct, gl, slopes, ovt, bsz, seq, n_cmp, n_sel, n_top):
    tq = TQ_CMP
    nq = seq // tq
    ncol = kc.shape[2]
    kern = functools.partial(_cmp_attn_kernel, n_cmp=n_cmp, n_sel=n_sel, n_top=n_top)
    return pl.pallas_call(
        kern,
        grid=(bsz, A_GROUPS, nq),
        in_specs=[
            pl.BlockSpec((tq, 2 * LANES), lambda b, g, i: (b * nq + i, g)),
            pl.BlockSpec((None, None, ncol, 2 * LANES), lambda b, g, i: (b, g, 0, 0)),
            pl.BlockSpec((None, None, A_DH, ncol), lambda b, g, i: (b, g, 0, 0)),
            pl.BlockSpec((tq, LANES), lambda b, g, i: (b * nq + i, g)),
            pl.BlockSpec((None, SUBLANES, LANES), lambda b, g, i: (g, 0, 0)),
            pl.BlockSpec(ovt.shape, lambda b, g, i: (0, 0)),
        ],
        out_specs=[
            pl.BlockSpec((tq, 2 * LANES), lambda b, g, i: (b * nq + i, g)),
            pl.BlockSpec((None, None, tq, LANES), lambda b, g, i: (b, g, i, 0)),
        ],
        out_shape=[
            jax.ShapeDtypeStruct((bsz * seq, A_WIDTH), BF16),
            jax.ShapeDtypeStruct((bsz, A_GROUPS, seq, LANES), BF16),
        ],
        compiler_params=_cparams(("parallel", "parallel", "parallel")),
        name="cmp_attention_topk",
    )(a0, kc, vct, gl, slopes, ovt)


def _stage_kv(kv_ref, ka_ref, vt_ref, row0, chunk, aug_fn):
    seq = kv_ref.shape[0]
    for t in range(seq // chunk):
        kvf = kv_ref[t * chunk:(t + 1) * chunk, :].astype(F32)
        low = lax.broadcasted_iota(jnp.int32, kvf.shape, 1) < A_DH
        rolled = pltpu.roll(kvf, A_DH, 1)
        r0 = row0 + t * chunk
        ka_ref[0, r0:r0 + chunk, 0:LANES] = jnp.where(low, kvf, 0.0).astype(BF16)
        ka_ref[1, r0:r0 + chunk, 0:LANES] = jnp.where(low, 0.0, rolled).astype(BF16)
        ka_ref[0, r0:r0 + chunk, LANES:2 * LANES] = aug_fn(t, 0)
        ka_ref[1, r0:r0 + chunk, LANES:2 * LANES] = aug_fn(t, 1)
        vt_ref[row0 // chunk + t] = kvf.T[A_DH:, :].astype(BF16)


def _bf16_pieces(x):
    p1 = x.astype(BF16)
    r1 = x - p1.astype(F32)
    p2 = r1.astype(BF16)
    p3 = (r1 - p2.astype(F32)).astype(BF16)
    return [p1.astype(F32), p2.astype(F32), p3.astype(F32)]


def _slc_kernel(q_ref, sb_ref, kv_ref, gl_ref, sl_ref, o_ref,
                ka_ref, vt_ref, ex_ref, bd_ref, m_ref, l_ref, acc_ref, qa_ref, sa_ref, sb2_ref):
    tq = q_ref.shape[0]
    tk = tq
    i = pl.program_id(2)

    lane = lax.broadcasted_iota(jnp.int32, (tq, LANES), 1)
    pos = lax.broadcasted_iota(jnp.int32, (tq, LANES), 0).astype(F32)

    @pl.when(i == 0)
    def _stage():
        def key_aug(t, e):
            blk = lax.shift_right_logical(lax.broadcasted_iota(jnp.int32, (tk, LANES), 0) + t * tk, SLC_SHIFT)
            aug = jnp.where(blk == lane, 1.0, 0.0)
            aug = jnp.where((lane >= A_DH + 3 * e) & (lane < A_DH + 3 * e + 3), pos, aug)
            aug = jnp.where((lane >= A_DH + 6 + 3 * e) & (lane < A_DH + 9 + 3 * e), -1.0, aug)
            return aug.astype(BF16)

        _stage_kv(kv_ref, ka_ref, vt_ref, 0, tk, key_aug)
        for p in range(2):
            ex = jnp.zeros((tq, LANES), F32)
            for e in range(2):
                slope = jnp.broadcast_to(sl_ref[2 * p + e:2 * p + e + 1, 0:1], (tq, LANES))
                for j, piece in enumerate(_bf16_pieces(slope)):
                    ex = jnp.where(lane == A_DH + 3 * e + j, piece, ex)
                for j, piece in enumerate(_bf16_pieces(slope * pos)):
                    ex = jnp.where(lane == A_DH + 6 + 3 * e + j, piece, ex)
            ex_ref[p] = ex.astype(BF16)
        kk = lax.broadcasted_iota(jnp.int32, (tk, tq), 0)
        qq = lax.broadcasted_iota(jnp.int32, (tk, tq), 1)
        bd_ref[...] = jnp.where(kk <= qq, 0.0, NEG)

    sb = sb_ref[...]
    for p in range(2):
        qa_ref[p, :, 0:LANES] = q_ref[:, p * LANES:(p + 1) * LANES]
        qa_ref[p, :, LANES:2 * LANES] = jnp.where(lane < A_DH, sb, ex_ref[p])
    m_ref[...] = jnp.full(m_ref.shape, NEG, F32)
    l_ref[...] = jnp.zeros(l_ref.shape, F32)
    acc_ref[...] = jnp.zeros(acc_ref.shape, F32)

    def scores(kt, s_ref):
        k0 = pl.multiple_of(kt * tk, tk)
        for hg in range(A_HG):
            s_ref[hg] = _dot_nt(ka_ref[hg % 2, pl.ds(k0, tk), :], qa_ref[hg // 2])

    def softmax_pv(kt, s_ref, causal=False):
        off = ((i - kt) * tk).astype(F32)
        vt = vt_ref[kt]
        m_all = m_ref[...]
        l_all = l_ref[...]
        probs, alphas, m_rows, l_rows = [], [], [], []
        for hg in range(A_HG):
            cst = -sl_ref[hg:hg + 1, 0:1] * off
            s = s_ref[hg] + bd_ref[...] if causal else s_ref[hg]
            m_prev = m_all[hg:hg + 1, :]
            m_new = jnp.maximum(m_prev, jnp.max(s, axis=0, keepdims=True) + cst)
            alpha = jnp.exp2(m_prev - m_new)
            pm = jnp.exp2(s - (m_new - cst))
            l_rows.append(alpha * l_all[hg:hg + 1, :] + jnp.sum(pm, axis=0, keepdims=True))
            m_rows.append(m_new)
            alphas.append(alpha)
            probs.append(pm.astype(BF16))
        pad = [jnp.zeros((SUBLANES - A_HG, tq), F32)]
        m_ref[...] = jnp.concatenate(m_rows + pad, axis=0)
        l_ref[...] = jnp.concatenate(l_rows + pad, axis=0)
        for hg in range(A_HG):
            acc_ref[hg] = acc_ref[hg] * alphas[hg] + _dot(vt, probs[hg])

    scores(0, sa_ref)

    def body(j, carry):
        kt = 2 * j
        scores(kt + 1, sb2_ref)
        softmax_pv(kt, sa_ref)
        scores(kt + 2, sa_ref)
        softmax_pv(kt + 1, sb2_ref)
        return carry

    lax.fori_loop(0, i // 2, body, 0)

    @pl.when(i % 2 == 0)
    def _():
        softmax_pv(i, sa_ref, causal=True)

    @pl.when(i % 2 == 1)
    def _():
        scores(i, sb2_ref)
        softmax_pv(i - 1, sa_ref)
        softmax_pv(i, sb2_ref, causal=True)

    gate_t = jax.nn.sigmoid(gl_ref[...]).T
    outs = []
    for hg in range(A_HG):
        w = gate_t[A_HG + hg:A_HG + hg + 1, :] / jnp.maximum(l_ref[hg:hg + 1, :], 1e-30)
        outs.append(acc_ref[hg] * w)
    o_ref[...] = jnp.concatenate(outs, axis=0).T.astype(o_ref.dtype)


def _slc_attn(a0, sb, gl, slopes, bsz, seq, kv_col):
    nq = seq // TQ
    return pl.pallas_call(
        _slc_kernel,
        grid=(bsz, A_GROUPS, nq),
        in_specs=[
            pl.BlockSpec((TQ, 2 * LANES), lambda b, g, i: (b * nq + i, g)),
            pl.BlockSpec((None, None, TQ, LANES), lambda b, g, i: (b, g, i, 0)),
            pl.BlockSpec((seq, LANES), lambda b, g, i: (b, kv_col + g)),
            pl.BlockSpec((TQ, LANES), lambda b, g, i: (b * nq + i, g)),
            pl.BlockSpec((None, SUBLANES, LANES), lambda b, g, i: (g, 0, 0)),
        ],
        out_specs=pl.BlockSpec((TQ, 2 * LANES), lambda b, g, i: (b * nq + i, g)),
        out_shape=jax.ShapeDtypeStruct((bsz * seq, A_WIDTH), BF16),
        scratch_shapes=[
            pltpu.VMEM((2, seq, 2 * LANES), BF16),
            pltpu.VMEM((seq // TQ, A_DH, TQ), BF16),
            pltpu.VMEM((2, TQ, LANES), BF16),
            pltpu.VMEM((TQ, TQ), F32),
            pltpu.VMEM((SUBLANES, TQ), F32),
            pltpu.VMEM((SUBLANES, TQ), F32),
            pltpu.VMEM((A_HG, A_DH, TQ), F32),
            pltpu.VMEM((2, TQ, 2 * LANES), BF16),
            pltpu.VMEM((A_HG, TQ, TQ), F32),
            pltpu.VMEM((A_HG, TQ, TQ), F32),
        ],
        compiler_params=_cparams(("parallel", "parallel", "arbitrary")),
        name="selected_attention",
    )(a0, sb, a0, gl, slopes)


def _win_kernel(q_ref, kv_ref, gl_ref, sl_ref, o_ref, ka_ref, vt_ref, bw_ref):
    tq = TQ
    nsub = q_ref.shape[0] // tq
    wk = WIN + tq
    npad = WIN // tq
    i = pl.program_id(2)

    @pl.when(i == 0)
    def _stage():
        lane_p = lax.broadcasted_iota(jnp.int32, (WIN, LANES), 1)
        for e in range(2):
            ka_ref[e, 0:WIN, 0:LANES] = jnp.zeros((WIN, LANES), BF16)
            ka_ref[e, 0:WIN, LANES:2 * LANES] = jnp.where(lane_p == 0, 1.0, 0.0).astype(BF16)
        for t in range(npad):
            vt_ref[t] = jnp.zeros(vt_ref.shape[1:], BF16)
        _stage_kv(kv_ref, ka_ref, vt_ref, WIN, tq, lambda t, e: jnp.zeros((tq, LANES), BF16))
        kk = lax.broadcasted_iota(jnp.int32, (wk, tq), 0)
        qq = lax.broadcasted_iota(jnp.int32, (wk, tq), 1)
        dist = qq + WIN - kk
        ok = (dist >= 0) & (dist < WIN)
        distf = dist.astype(F32)
        for hg in range(A_HG):
            bw_ref[hg] = jnp.where(ok, -sl_ref[hg:hg + 1, 0:1] * distf, NEG)

    lane = lax.broadcasted_iota(jnp.int32, (tq, LANES), 1)
    negrow = jnp.where(lane == 0, NEG, 0.0).astype(BF16)
    gate_t = jax.nn.sigmoid(gl_ref[...]).T
    units = [(sub, hg) for sub in range(nsub) for hg in range(A_HG)]
    scores = []
    for sub in range(nsub):
        k0 = pl.multiple_of((i * nsub + sub) * tq, tq)
        rows = slice(sub * tq, (sub + 1) * tq)
        qa = [jnp.concatenate([q_ref[rows, p * LANES:(p + 1) * LANES], negrow], axis=1) for p in range(2)]
        scores += [_dot_nt(ka_ref[hg % 2, pl.ds(k0, wk), :], qa[hg // 2]) for hg in range(A_HG)]
    probs, scales = [], []
    for u, (sub, hg) in enumerate(units):
        s = scores[u] + bw_ref[hg]
        pm = jnp.exp2(s - jnp.max(s, axis=0, keepdims=True))
        l = jnp.sum(pm, axis=0, keepdims=True)
        gate = gate_t[2 * A_HG + hg:2 * A_HG + hg + 1, sub * tq:(sub + 1) * tq]
        scales.append(gate / jnp.maximum(l, 1e-30))
        probs.append(pm.astype(BF16))
    outs = []
    for u, (sub, hg) in enumerate(units):
        t0 = i * nsub + sub
        o = _dot(vt_ref[t0], probs[u][0:tq, :])
        for j in range(1, wk // tq):
            o = o + _dot(vt_ref[t0 + j], probs[u][j * tq:(j + 1) * tq, :])
        outs.append(o * scales[u])
    for sub in range(nsub):
        tile = jnp.concatenate(outs[sub * A_HG:(sub + 1) * A_HG], axis=0).T
        o_ref[sub * tq:(sub + 1) * tq, :] = tile.astype(o_ref.dtype)


def _win_attn(a0, gl, slopes, bsz, seq, kv_col):
    tb = WIN_SUBTILES * TQ
    nq = seq // tb
    return pl.pallas_call(
        _win_kernel,
        grid=(bsz, A_GROUPS, nq),
        in_specs=[
            pl.BlockSpec((tb, 2 * LANES), lambda b, g, i: (b * nq + i, g)),
            pl.BlockSpec((seq, LANES), lambda b, g, i: (b, kv_col + g)),
            pl.BlockSpec((tb, LANES), lambda b, g, i: (b * nq + i, g)),
            pl.BlockSpec((None, SUBLANES, LANES), lambda b, g, i: (g, 0, 0)),
        ],
        out_specs=pl.BlockSpec((tb, 2 * LANES), lambda b, g, i: (b * nq + i, g)),
        out_shape=jax.ShapeDtypeStruct((bsz * seq, A_WIDTH), BF16),
        scratch_shapes=[
            pltpu.VMEM((2, WIN + seq, 2 * LANES), BF16),
            pltpu.VMEM(((WIN + seq) // TQ, A_DH, TQ), BF16),
            pltpu.VMEM((A_HG, WIN + TQ, TQ), F32),
        ],
        compiler_params=_cparams(("parallel", "parallel", "arbitrary")),
        name="window_attention",
    )(a0, a0, gl, slopes)


def _outproj0_kernel(oc_ref, os_ref, ow_ref, z_ref, x_ref, gate_ref, w_ref, o_ref):
    o = oc_ref[...].astype(F32) + os_ref[...].astype(F32) + ow_ref[...].astype(F32)
    y = (o * _silu(z_ref[...].astype(F32))).astype(BF16)
    o_ref[...] = x_ref[...] + gate_ref[...] * _dot(y, w_ref[...])


def _outproj0(oc, osl, ow, a0, x2d, gate, w, seq, z_col):
    m, d = x2d.shape
    tm = TM_OUT
    per_b = seq // tm
    row = lambda i: (i, 0)
    return pl.pallas_call(
        _outproj0_kernel,
        grid=(m // tm,),
        in_specs=[
            pl.BlockSpec((tm, A_WIDTH), row),
            pl.BlockSpec((tm, A_WIDTH), row),
            pl.BlockSpec((tm, A_WIDTH), row),
            pl.BlockSpec((tm, A_WIDTH), lambda i: (i, z_col)),
            pl.BlockSpec((tm, d), row),
            pl.BlockSpec((None, 1, d), lambda i: (i // per_b, 0, 0)),
            pl.BlockSpec(w.shape, lambda i: (0, 0)),
        ],
        out_specs=pl.BlockSpec((tm, d), row),
        out_shape=jax.ShapeDtypeStruct((m, d), F32),
        compiler_params=_cparams(("parallel",)),
        name="nsa_out_proj",
    )(oc, osl, ow, a0, x2d, gate[:, None, :], w)


def _conv_kernel(x_ref, halo_ref, w_ref, b_ref, ks_ref, o_ref, *, per_b):
    first = (pl.program_id(0) % per_b) == 0
    halo = jnp.where(first, 0.0, halo_ref[...].astype(F32))
    xe = jnp.concatenate([halo, x_ref[...].astype(F32)], axis=0)
    y = xe[SUBLANES:, :] * w_ref[CONV_W - 1:CONV_W, :] + b_ref[...]
    for s in range(1, CONV_W):
        y = y + pltpu.roll(xe, s, 0)[SUBLANES:, :] * w_ref[CONV_W - 1 - s:CONV_W - s, :]
    o_ref[...] = (_silu(y) * ks_ref[...]).astype(o_ref.dtype)


def _conv_silu(a1, conv_w, conv_b, kscale, seq):
    m = a1.shape[0]
    tm, tc = TM_CONV, TC_CONV
    per_b = seq // tm
    hb = tm // SUBLANES
    kern = functools.partial(_conv_kernel, per_b=per_b)
    return pl.pallas_call(
        kern,
        grid=(m // tm, B_QK // tc),
        in_specs=[
            pl.BlockSpec((tm, tc), lambda i, j: (i, j)),
            pl.BlockSpec((SUBLANES, tc), lambda i, j: (jnp.maximum(i * hb - 1, 0), j)),
            pl.BlockSpec((CONV_W, tc), lambda i, j: (0, j)),
            pl.BlockSpec((1, tc), lambda i, j: (0, j)),
            pl.BlockSpec((1, tc), lambda i, j: (0, j)),
        ],
        out_specs=pl.BlockSpec((tm, tc), lambda i, j: (i, j)),
        out_shape=jax.ShapeDtypeStruct((m, B_QK), BF16),
        compiler_params=_cparams(("parallel", "parallel")),
        name="causal_conv_silu",
    )(a1, a1, conv_w, conv_b.reshape(1, B_QK), kscale)


def _mlstm_kernel(q_ref, k_ref, v_ref, g_ref, gb_ref, hg_ref, o_ref, c_ref, n_ref, m_ref):
    ln = q_ref.shape[0]
    nh = B_HEADS

    @pl.when(pl.program_id(1) == 0)
    def _():
        c_ref[...] = jnp.zeros(c_ref.shape, F32)
        n_ref[...] = jnp.zeros(n_ref.shape, F32)
        m_ref[...] = jnp.zeros(m_ref.shape, F32)

    lane = lax.broadcasted_iota(jnp.int32, (ln, LANES), 1)
    gi = (g_ref[:, 0:LANES] + gb_ref[:, 0:LANES]) * LOG2E
    gf = g_ref[:, LANES:2 * LANES] + gb_ref[:, LANES:2 * LANES]
    lf = (jnp.minimum(gf, 0.0) - jnp.log1p(jnp.exp(-jnp.abs(gf)))) * LOG2E
    r = lax.broadcasted_iota(jnp.int32, (ln, ln), 0)
    c = lax.broadcasted_iota(jnp.int32, (ln, ln), 1)
    causal = c <= r
    bcum = _dot(jnp.where(causal, 1.0, 0.0), lf, precision=HIGHEST)
    rel_t = (gi - bcum).T
    pm = rel_t[0:SUBLANES, :]
    lane_r = lax.broadcasted_iota(jnp.int32, pm.shape, 1)
    step = 1
    while step < ln:
        pm = jnp.maximum(pm, jnp.where(lane_r >= step, pltpu.roll(pm, step, 1), NEG))
        step *= 2
    pmax = jnp.concatenate([pm, jnp.zeros((LANES - SUBLANES, ln), F32)], axis=0).T
    m_st = m_ref[...]
    g_tot = bcum[ln - 1:ln, :]
    m_inter = bcum + m_st
    m_j = jnp.maximum(bcum + pmax, m_inter)
    e_arg = bcum - m_j
    w_inter = jnp.exp2(m_inter - m_j)
    a_all = g_tot - bcum + gi
    m_loc = jnp.max(a_all, axis=0, keepdims=True)
    w_all = jnp.exp2(a_all - m_loc)
    m_new = jnp.maximum(g_tot + m_st, m_loc)
    s_old = jnp.exp2(g_tot + m_st - m_new)
    s_new = jnp.exp2(m_loc - m_new)
    m_ref[...] = m_new

    def head(ref, h, width):
        return ref[:, h * width:(h + 1) * width]

    def onehot_col(h):
        return jnp.where(lane == h, 1.0, 0.0).astype(BF16)

    sqk = [_dot_nt(head(q_ref, h, B_DK), head(k_ref, h, B_DK)) for h in range(nh)]
    qc = [_dot(head(q_ref, h, B_DK), c_ref[h].astype(BF16)) for h in range(nh)]
    qn = _dot(head(q_ref, 0, B_DK), n_ref[0].astype(BF16))
    for h in range(1, nh):
        qn = qn + _dot(head(q_ref, h, B_DK), n_ref[h].astype(BF16))

    smats, kwts = [], []
    rowsum = jnp.zeros((ln, LANES), F32)
    for h in range(nh):
        log_w = jnp.where(causal, e_arg[:, h:h + 1] + rel_t[h:h + 1, :], NEG)
        smat = sqk[h] * jnp.exp2(log_w)
        rowsum = jnp.where(lane == h, jnp.sum(smat, axis=1, keepdims=True), rowsum)
        smats.append(smat.astype(BF16))
        kw = head(k_ref, h, B_DK).astype(F32) * w_all[:, h:h + 1]
        kwts.append(kw.T.astype(BF16))

    intra = [_dot(smats[h], head(v_ref, h, B_DV)) for h in range(nh)]
    c_loc = [_dot(kwts[h], head(v_ref, h, B_DV)) for h in range(nh)]
    n_loc = [_dot(kwts[h], onehot_col(h)) for h in range(nh)]

    us = []
    msq = jnp.zeros((ln, LANES), F32)
    for h in range(nh):
        c_ref[h] = s_old[:, h:h + 1] * c_ref[h] + s_new[:, h:h + 1] * c_loc[h]
        n_ref[h] = s_old[:, h:h + 1] * n_ref[h] + s_new[:, h:h + 1] * n_loc[h]
        u = intra[h] + w_inter[:, h:h + 1] * qc[h]
        msq = jnp.where(lane == h, jnp.mean(u * u, axis=-1, keepdims=True), msq)
        us.append(u)
    den = rowsum + w_inter * qn
    scale = 1.0 / jnp.maximum(jnp.abs(den), jnp.exp2(-m_j))
    f = scale * lax.rsqrt(scale * scale * msq + EPS)
    for h in range(nh):
        o_ref[:, h * B_DV:(h + 1) * B_DV] = (us[h] * f[:, h:h + 1] * head(hg_ref, h, B_DV)).astype(o_ref.dtype)


def _mlstm(qk, a1, g1, gbias, head_g, bsz, seq):
    ln = ML_CHUNK
    nc = seq // ln
    wq = B_QK // 2
    return pl.pallas_call(
        _mlstm_kernel,
        grid=(bsz, nc),
        in_specs=[
            pl.BlockSpec((ln, wq), lambda b, n: (b * nc + n, 0)),
            pl.BlockSpec((ln, wq), lambda b, n: (b * nc + n, 1)),
            pl.BlockSpec((ln, B_WIDTH), lambda b, n: (b * nc + n, B_QK // B_WIDTH)),
            pl.BlockSpec((ln, 2 * LANES), lambda b, n: (b * nc + n, 0)),
            pl.BlockSpec((1, 2 * LANES), lambda b, n: (0, 0)),
            pl.BlockSpec((1, B_WIDTH), lambda b, n: (0, 0)),
        ],
        out_specs=pl.BlockSpec((ln, B_WIDTH), lambda b, n: (b * nc + n, 0)),
        out_shape=jax.ShapeDtypeStruct((bsz * seq, B_WIDTH), BF16),
        scratch_shapes=[
            pltpu.VMEM((B_HEADS, B_DK, B_DV), F32),
            pltpu.VMEM((B_HEADS, B_DK, LANES), F32),
            pltpu.VMEM((1, LANES), F32),
        ],
        compiler_params=_cparams(("parallel", "arbitrary")),
        name="mlstm_chunkwise",
    )(qk, qk, a1, g1, gbias, head_g.reshape(1, B_WIDTH))


def _outproj1_kernel(h_ref, og_ref, z_ref, x_ref, gate_ref, w_ref, fg_ref, o_ref):
    y = jax.nn.sigmoid(og_ref[...].astype(F32)) * h_ref[...].astype(F32) * _silu(z_ref[...].astype(F32))
    x2 = x_ref[...] + gate_ref[...] * _dot(y.astype(BF16), w_ref[...])
    o_ref[...] = x2 * lax.rsqrt(jnp.mean(x2 * x2, axis=-1, keepdims=True) + EPS) * fg_ref[...]


def _outproj1(hn, a1, x2d, gate, w, final_g, seq):
    m, d = x2d.shape
    tm = TM_OUT
    per_b = seq // tm
    row = lambda i: (i, 0)
    og_col = B_QK // B_WIDTH + 1
    return pl.pallas_call(
        _outproj1_kernel,
        grid=(m // tm,),
        in_specs=[
            pl.BlockSpec((tm, B_WIDTH), row),
            pl.BlockSpec((tm, B_WIDTH), lambda i: (i, og_col)),
            pl.BlockSpec((tm, B_WIDTH), lambda i: (i, og_col + 1)),
            pl.BlockSpec((tm, d), row),
            pl.BlockSpec((None, 1, d), lambda i: (i // per_b, 0, 0)),
            pl.BlockSpec(w.shape, lambda i: (0, 0)),
            pl.BlockSpec((1, d), lambda i: (0, 0)),
        ],
        out_specs=pl.BlockSpec((tm, d), row),
        out_shape=jax.ShapeDtypeStruct((m, d), F32),
        compiler_params=_cparams(("parallel",)),
        name="mlstm_out_proj_final_norm",
    )(hn, a1, a1, x2d, gate[:, None, :], w, final_g.reshape(1, d))


def _alibi_slopes():
    return np.asarray(2.0 ** (-8.0 * np.arange(1, A_HEADS + 1) / A_HEADS), np.float32)


def _selection_overlap_t(seq, cols):
    n_cmp = seq // CMP_STRIDE - CMP_LEN // CMP_STRIDE + 1
    n_sel = seq // SLC_LEN
    c0 = np.arange(n_cmp)[None, :] * CMP_STRIDE
    s0 = np.arange(n_sel)[:, None] * SLC_LEN
    ov = np.clip(np.minimum(c0 + CMP_LEN, s0 + SLC_LEN) - np.maximum(c0, s0), 0, None) / CMP_LEN
    out = np.zeros((n_sel, cols), np.float32)
    out[:, :n_cmp] = ov
    return out


def _layer0_weights(w_in):
    d = w_in.shape[0]
    kv0 = A_WIDTH
    gl0 = kv0 + 6 * A_GROUPS * A_DH
    z0 = gl0 + 3 * A_HEADS
    wq = w_in[:, :A_WIDTH] * (A_DH ** -0.5 * LOG2E)
    wz = w_in[:, z0:z0 + A_WIDTH]
    wkv = w_in[:, kv0:gl0].reshape(d, 6, A_GROUPS, A_DH)
    branches = [wkv[:, 2 * r:2 * r + 2].transpose(0, 2, 1, 3).reshape(d, A_GROUPS * 2 * A_DH) for r in range(3)]
    w0 = jnp.concatenate([wq, wz] + branches, axis=1).astype(BF16)
    wgl = w_in[:, gl0:z0].reshape(d, 3, A_GROUPS, A_HG).transpose(0, 2, 1, 3).reshape(d, A_GROUPS, 3 * A_HG)
    wgl = jnp.pad(wgl, ((0, 0), (0, 0), (0, LANES - 3 * A_HG))).reshape(d, A_GROUPS * LANES).astype(BF16)
    return w0, wgl


def _compress_weights(pe, w1, b1, w2, b2):
    half = CMP_LEN // 2
    w1r = w1.reshape(2, 2, half, A_DH, CMP_HIDDEN)
    wc = jnp.einsum("khldn,kq->lkdqhn", w1r, jnp.eye(2, dtype=w1.dtype))
    wc = wc.reshape(half * 2 * A_DH, 4 * CMP_HIDDEN).astype(BF16)
    pec = pe.reshape(2, 2, half, A_DH).transpose(1, 2, 0, 3).reshape(2, half * 2 * A_DH)
    pec = jnp.pad(pec, ((0, SUBLANES - 2), (0, 0))).astype(BF16)
    b1c = b1.reshape(1, 2 * CMP_HIDDEN)
    w2k = jnp.zeros((CMP_HIDDEN, 2 * LANES), F32)
    w2k = w2k.at[:, 0:A_DH].set(w2[0]).at[:, LANES + A_DH:2 * LANES].set(w2[0])
    b2k = jnp.zeros((1, 2 * LANES), F32)
    b2k = b2k.at[0, 0:A_DH].set(b2[0]).at[0, LANES + A_DH:2 * LANES].set(b2[0])
    w2vt = w2[1].T.astype(BF16)
    b2v = jnp.broadcast_to(b2[1][:, None], (A_DH, LANES))
    return wc, pec, b1c, w2k.astype(BF16), b2k, w2vt, b2v


def _layer1_weights(w_in):
    g0 = B_QK + B_WIDTH
    w_main = jnp.concatenate([w_in[:, :g0], w_in[:, g0 + 2 * B_HEADS:]], axis=1).astype(BF16)
    pad = ((0, 0), (0, LANES - B_HEADS))
    wg = jnp.concatenate([jnp.pad(w_in[:, g0:g0 + B_HEADS], pad),
                          jnp.pad(w_in[:, g0 + B_HEADS:g0 + 2 * B_HEADS], pad)], axis=1).astype(BF16)
    return w_main, wg


def kernel(x, c, ada_w, ada_b, norm_g, final_g, a_w_in, a_cmp_pe, a_cmp_w1, a_cmp_b1, a_cmp_w2, a_cmp_b2,
           a_w_out, b_w_in, b_conv_w, b_conv_b, b_gate_b, b_head_g, b_w_out):
    bsz, seq, d = x.shape
    assert d == D_MODEL and seq % (2 * TM_PROJ) == 0 and ada_w.shape[0] == 2
    m = bsz * seq
    x2d = x.reshape(m, d)
    mod = _ada_mod(c, ada_w, ada_b)

    w0, wgl = _layer0_weights(a_w_in[0])
    a0, gl = _normmod_matmul(x2d, norm_g[0], mod[0, :, d:2 * d], mod[0, :, :d], w0, wgl, seq, TN_PROJ0, "nsa_in_proj")
    z_col = 1
    kv_col = 2 * A_WIDTH // LANES
    n_chunk = seq // CMP_STRIDE
    n_cmp = n_chunk - CMP_LEN // CMP_STRIDE + 1
    n_sel = seq // SLC_LEN
    n_top = min(SLC_TOP, n_sel)
    assert n_sel % SUBLANES == 0 and n_sel <= LANES
    kvsrc = a0[:, 2 * A_WIDTH:2 * A_WIDTH + A_GROUPS * LANES]
    xc = kvsrc.reshape(bsz, n_chunk, CMP_STRIDE, A_GROUPS, LANES).transpose(0, 3, 1, 2, 4)
    xc = xc.reshape(bsz, A_GROUPS, n_chunk, CMP_STRIDE * LANES)
    kc, vct = _compress(xc, *_compress_weights(a_cmp_pe[0], a_cmp_w1[0], a_cmp_b1[0], a_cmp_w2[0], a_cmp_b2[0]))
    slopes = jnp.asarray(np.broadcast_to(
        np.pad((_alibi_slopes() * np.float32(LOG2E)).reshape(A_GROUPS, A_HG), ((0, 0), (0, SUBLANES - A_HG)))[:, :, None],
        (A_GROUPS, SUBLANES, LANES)))
    ovt = jnp.asarray(_selection_overlap_t(seq, n_chunk))
    o_cmp, sb = _cmp_attn(a0, kc, vct, gl, slopes, ovt, bsz, seq, n_cmp, n_sel, n_top)
    o_slc = _slc_attn(a0, sb, gl, slopes, bsz, seq, kv_col + A_GROUPS)
    o_win = _win_attn(a0, gl, slopes, bsz, seq, kv_col + 2 * A_GROUPS)
    x1 = _outproj0(o_cmp, o_slc, o_win, a0, x2d, mod[0, :, 2 * d:], a_w_out[0].astype(BF16), seq, z_col)

    w1m, w1g = _layer1_weights(b_w_in[0])
    a1, g1 = _normmod_matmul(x1, norm_g[1], mod[1, :, d:2 * d], mod[1, :, :d], w1m, w1g, seq, TN_PROJ1, "mlstm_in_proj")
    kscale = jnp.concatenate([jnp.ones((1, B_QK // 2), F32), jnp.full((1, B_QK // 2), B_DK ** -0.5, F32)], axis=1)
    qk = _conv_silu(a1, b_conv_w[0], b_conv_b[0], kscale, seq)
    gbias = jnp.pad(b_gate_b[0], ((0, 0), (0, LANES - B_HEADS))).reshape(1, 2 * LANES)
    y1 = _mlstm(qk, a1, g1, gbias, b_head_g[0], bsz, seq)
    out = _outproj1(y1, a1, x1, mod[1, :, 2 * d:], b_w_out[0].astype(BF16), final_g, seq)
    return out.reshape(bsz, seq, d)
```

```python
import functools

import numpy as np
import jax
import jax.numpy as jnp
from jax import lax
from jax.experimental import pallas as pl
from jax.experimental.pallas import tpu as pltpu

F32 = jnp.float32
BF16 = jnp.bfloat16
HIGHEST = lax.Precision.HIGHEST

EPS = 1e-6
NEG = -1e30
LOG2E = float(np.log2(np.e))

D_MODEL = 1024
A_HEADS = 16
A_GROUPS = 4
A_HG = A_HEADS // A_GROUPS
A_DH = 64
A_WIDTH = A_HEADS * A_DH
CMP_LEN = 32
CMP_STRIDE = 16
CMP_HIDDEN = 256
SLC_LEN = 64
SLC_SHIFT = 6
SLC_TOP = 16
WIN = 512
B_HEADS = 8
B_DK = 128
B_DV = 256
B_WIDTH = B_HEADS * B_DV
B_QK = 2 * B_HEADS * B_DK
CONV_W = 4

LANES = 128
SUBLANES = 8
VMEM_LIMIT = 56 * 1024 * 1024

TQ = 256
TQ_SLC = 512
TQ_CMP = 1024
WIN_SUBTILES = 2
VT_ROWS = A_DH + 16
ML_CHUNK = 256
TM_PROJ = 1024
TN_PROJ0 = 1792
TN_PROJ1 = 2048
TM_OUT = 512
TM_CONV = 1024
TC_CONV = 512

assert WIN % TQ == 0 and 1 << SLC_SHIFT == SLC_LEN


def _dot(a, b, **kw):
    return jnp.dot(a, b, preferred_element_type=F32, **kw)


def _dot_nt(a, b):
    return lax.dot_general(a, b, (((1,), (1,)), ((), ())), preferred_element_type=F32)


def _cparams(sem):
    return pltpu.CompilerParams(dimension_semantics=sem, vmem_limit_bytes=VMEM_LIMIT)


def _silu(x):
    return x * jax.nn.sigmoid(x)


def _ada_kernel(c_ref, w_ref, b_ref, o_ref):
    o_ref[...] = _dot(c_ref[...], w_ref[...], precision=HIGHEST) + b_ref[...]


def _ada_mod(c, ada_w, ada_b):
    depth, d, d3 = ada_w.shape
    bsz = c.shape[0]
    nj = d3 // d
    return pl.pallas_call(
        _ada_kernel,
        grid=(depth, nj),
        in_specs=[
            pl.BlockSpec((bsz, d), lambda i, j: (0, 0)),
            pl.BlockSpec((None, d, d), lambda i, j: (i, 0, j)),
            pl.BlockSpec((None, 1, d), lambda i, j: (i, 0, j)),
        ],
        out_specs=pl.BlockSpec((None, bsz, d), lambda i, j: (i, 0, j)),
        out_shape=jax.ShapeDtypeStruct((depth, bsz, d3), F32),
        compiler_params=_cparams(("parallel", "parallel")),
        name="ada_mod",
    )(c, ada_w, ada_b.reshape(depth, 1, d3))


def _normmod_kernel(x_ref, g_ref, sc_ref, sh_ref, w_ref, wg_ref, o_ref, og_ref, h_ref):
    @pl.when(pl.program_id(1) == 0)
    def _():
        x = x_ref[...]
        xn = x * lax.rsqrt(jnp.mean(x * x, axis=-1, keepdims=True) + EPS)
        h = (xn * g_ref[...]) * (1.0 + sc_ref[...]) + sh_ref[...]
        hb = h.astype(BF16)
        h_ref[...] = hb
        half = hb.shape[0] // 2
        og_ref[0:half, :] = _dot(hb[0:half, :], wg_ref[...])
        og_ref[half:, :] = _dot(hb[half:, :], wg_ref[...])

    o_ref[...] = _dot(h_ref[...], w_ref[...]).astype(o_ref.dtype)


def _normmod_matmul(x2d, g, scale, shift, w, wg, seq, tn, name):
    m, d = x2d.shape
    n = w.shape[1]
    ng = wg.shape[1]
    tm = TM_PROJ
    assert m % tm == 0 and n % tn == 0 and seq % tm == 0
    per_b = seq // tm
    w_tiles = w.reshape(d, n // tn, tn).transpose(1, 0, 2)
    return pl.pallas_call(
        _normmod_kernel,
        grid=(m // tm, n // tn),
        in_specs=[
            pl.BlockSpec((tm, d), lambda i, j: (i, 0)),
            pl.BlockSpec((1, d), lambda i, j: (0, 0)),
            pl.BlockSpec((None, 1, d), lambda i, j: (i // per_b, 0, 0)),
            pl.BlockSpec((None, 1, d), lambda i, j: (i // per_b, 0, 0)),
            pl.BlockSpec((None, d, tn), lambda i, j: (j, 0, 0)),
            pl.BlockSpec((d, ng), lambda i, j: (0, 0)),
        ],
        out_specs=[
            pl.BlockSpec((tm, tn), lambda i, j: (i, j)),
            pl.BlockSpec((tm, ng), lambda i, j: (i, 0)),
        ],
        out_shape=[jax.ShapeDtypeStruct((m, n), BF16), jax.ShapeDtypeStruct((m, ng), F32)],
        scratch_shapes=[pltpu.VMEM((tm, d), BF16)],
        compiler_params=_cparams(("parallel", "arbitrary")),
        name=name,
    )(x2d, g.reshape(1, d), scale[:, None, :], shift[:, None, :], w_tiles, wg)


def _compress_kernel(x_ref, wc_ref, pe_ref, b1_ref, w2k_ref, b2k_ref, w2vt_ref, b2v_ref, ok_ref, ov_ref):
    a = _dot(x_ref[...], wc_ref[...])
    pb = _dot(pe_ref[...], wc_ref[...])
    n = a.shape[0]
    hid = CMP_HIDDEN
    hs = []
    for kv in range(2):
        c0 = kv * 2 * hid
        first = a[:, c0:c0 + hid]
        second = pltpu.roll(a[:, c0 + hid:c0 + 2 * hid], n - 1, 0)
        bias = pb[0:1, c0:c0 + hid] + pb[1:2, c0 + hid:c0 + 2 * hid] + b1_ref[:, kv * hid:(kv + 1) * hid]
        hs.append(jax.nn.gelu(first + second + bias))
    ok_ref[...] = (_dot(hs[0].astype(BF16), w2k_ref[...]) + b2k_ref[...]).astype(ok_ref.dtype)
    ov_ref[...] = (_dot_nt(w2vt_ref[...], hs[1].astype(BF16)) + b2v_ref[:, 0:1]).astype(ov_ref.dtype)


def _compress(xc, wc, pec, b1c, w2k, b2k, w2vt, b2v):
    bsz, g, nchunk, kdim = xc.shape
    full = lambda a: pl.BlockSpec(a.shape, lambda b, gi: (0,) * a.ndim)
    return pl.pallas_call(
        _compress_kernel,
        grid=(bsz, g),
        in_specs=[pl.BlockSpec((None, None, nchunk, kdim), lambda b, gi: (b, gi, 0, 0)),
                  full(wc), full(pec), full(b1c), full(w2k), full(b2k), full(w2vt), full(b2v)],
        out_specs=[
            pl.BlockSpec((None, None, nchunk, 2 * LANES), lambda b, gi: (b, gi, 0, 0)),
            pl.BlockSpec((None, None, A_DH, nchunk), lambda b, gi: (b, gi, 0, 0)),
        ],
        out_shape=[
            jax.ShapeDtypeStruct((bsz, g, nchunk, 2 * LANES), BF16),
            jax.ShapeDtypeStruct((bsz, g, A_DH, nchunk), BF16),
        ],
        compiler_params=_cparams(("parallel", "parallel")),
        name="compress_tokens",
    )(xc, wc, pec, b1c, w2k, b2k, w2vt, b2v)


def _cmp_attn_kernel(q_ref, k_ref, vt_ref, gl_ref, sl_ref, ovt_ref, o_ref, sb_ref, *, n_cmp, n_sel, n_top):
    tq = q_ref.shape[0]
    ncol = k_ref.shape[0]
    q0 = pl.program_id(2) * tq
    tpos = lax.broadcasted_iota(jnp.int32, (ncol, tq), 1) + q0
    cidx = lax.broadcasted_iota(jnp.int32, (ncol, tq), 0)
    dist = tpos - (cidx * CMP_STRIDE + (CMP_LEN - 1))
    valid = (dist >= 0) & (cidx < n_cmp)
    distf = dist.astype(F32)
    gate_t = jax.nn.sigmoid(gl_ref[...]).T
    vt = vt_ref[...]
    psum = jnp.zeros((ncol, tq), F32)
    scores = [_dot_nt(k_ref[:, (hg % 2) * LANES:(hg % 2 + 1) * LANES], q_ref[:, (hg // 2) * LANES:(hg // 2 + 1) * LANES])
              for hg in range(A_HG)]
    probs = []
    for hg in range(A_HG):
        s = jnp.where(valid, scores[hg] - sl_ref[hg:hg + 1, 0:1] * distf, NEG)
        m = jnp.max(s, axis=0, keepdims=True)
        m = jnp.where(m > 0.5 * NEG, m, 0.0)
        pr = jnp.exp2(s - m)
        pr = pr * (1.0 / jnp.maximum(jnp.sum(pr, axis=0, keepdims=True), 1e-30))
        psum = psum + pr
        probs.append(pr.astype(BF16))
    outs = [_dot(vt, probs[hg]) * gate_t[hg:hg + 1, :] for hg in range(A_HG)]
    o_ref[...] = jnp.concatenate(outs, axis=0).T.astype(o_ref.dtype)

    imp = _dot(ovt_ref[...], psum, precision=HIGHEST)
    blk = lax.broadcasted_iota(jnp.int32, (n_sel, tq), 0)
    blkf = blk.astype(F32)
    cur = lax.shift_right_logical(lax.broadcasted_iota(jnp.int32, (n_sel, tq), 1) + q0, SLC_SHIFT)
    forced = (blk == 0) | (blk == cur) | (blk == cur - 1)
    score = jnp.where(forced, 3e38, jnp.where(blk <= cur, imp, -1.0))
    sel = jnp.zeros((n_sel, tq), F32)
    for _ in range(n_top):
        mx = jnp.max(score, axis=0, keepdims=True)
        idx = jnp.min(jnp.where(score == mx, blkf, 1e9), axis=0, keepdims=True)
        pick = blkf == idx
        sel = jnp.where(pick, 1.0, sel)
        score = jnp.where(pick, -3.0, score)
    bias_t = jnp.where(sel > 0.0, 0.0, NEG)
    bias_t = jnp.concatenate([bias_t, jnp.zeros((LANES - n_sel, tq), F32)], axis=0)
    sb_ref[...] = bias_t.T.astype(sb_ref.dtype)


def _cmp_attn(a0, kc, vct, gl, slopes, ovt, bsz, seq, n_cmp, n_sel, n_top):
    tq = TQ_CMP
    nq = seq // tq
    ncol = kc.shape[2]
    kern = functools.partial(_cmp_attn_kernel, n_cmp=n_cmp, n_sel=n_sel, n_top=n_top)
    return pl.pallas_call(
        kern,
        grid=(bsz, A_GROUPS, nq),
        in_specs=[
            pl.BlockSpec((tq, 2 * LANES), lambda b, g, i: (b * nq + i, g)),
            pl.BlockSpec((None, None, ncol, 2 * LANES), lambda b, g, i: (b, g, 0, 0)),
            pl.BlockSpec((None, None, A_DH, ncol), lambda b, g, i: (b, g, 0, 0)),
            pl.BlockSpec((tq, LANES), lambda b, g, i: (b * nq + i, g)),
            pl.BlockSpec((None, SUBLANES, LANES), lambda b, g, i: (g, 0, 0)),
            pl.BlockSpec(ovt.shape, lambda b, g, i: (0, 0)),
        ],
        out_specs=[
            pl.BlockSpec((tq, 2 * LANES), lambda b, g, i: (b * nq + i, g)),
            pl.BlockSpec((None, None, tq, LANES), lambda b, g, i: (b, g, i, 0)),
        ],
        out_shape=[
            jax.ShapeDtypeStruct((bsz * seq, A_WIDTH), BF16),
            jax.ShapeDtypeStruct((bsz, A_GROUPS, seq, LANES), BF16),
        ],
        compiler_params=_cparams(("parallel", "parallel", "parallel")),
        name="cmp_attention_topk",
    )(a0, kc, vct, gl, slopes, ovt)


def _stage_kv(kv_ref, ka_ref, vt_ref, row0, chunk, aug_fn):
    seq = kv_ref.shape[0]
    for t in range(seq // chunk):
        kvf = kv_ref[t * chunk:(t + 1) * chunk, :].astype(F32)
        low = lax.broadcasted_iota(jnp.int32, kvf.shape, 1) < A_DH
        rolled = pltpu.roll(kvf, A_DH, 1)
        r0 = row0 + t * chunk
        ka_ref[0, r0:r0 + chunk, 0:LANES] = jnp.where(low, kvf, 0.0).astype(BF16)
        ka_ref[1, r0:r0 + chunk, 0:LANES] = jnp.where(low, 0.0, rolled).astype(BF16)
        ka_ref[0, r0:r0 + chunk, LANES:2 * LANES] = aug_fn(t, 0)
        ka_ref[1, r0:r0 + chunk, LANES:2 * LANES] = aug_fn(t, 1)
        vt_ref[row0 // chunk + t, 0:A_DH, :] = kvf.T[A_DH:, :].astype(BF16)
        ones_rows = lax.broadcasted_iota(jnp.int32, (VT_ROWS - A_DH, chunk), 0) == 0
        vt_ref[row0 // chunk + t, A_DH:VT_ROWS, :] = jnp.where(ones_rows, 1.0, 0.0).astype(BF16)


def _bf16_pieces(x):
    p1 = x.astype(BF16)
    r1 = x - p1.astype(F32)
    p2 = r1.astype(BF16)
    p3 = (r1 - p2.astype(F32)).astype(BF16)
    return [p1.astype(F32), p2.astype(F32), p3.astype(F32)]


def _slc_kernel(q_ref, sb_ref, kv_ref, gl_ref, sl_ref, o_ref,
                ka_ref, vt_ref, ex_ref, bd_ref, m_ref, acc_ref, qa_ref, sa_ref, sb2_ref):
    tq = q_ref.shape[0]
    tk = tq
    i = pl.program_id(2)

    lane = lax.broadcasted_iota(jnp.int32, (tq, LANES), 1)
    pos = lax.broadcasted_iota(jnp.int32, (tq, LANES), 0).astype(F32)
    pos_hi = jnp.floor(pos * (1.0 / 256.0)) * 256.0
    key_pieces = [pos] if tk <= 256 else [pos_hi, pos - pos_hi]
    nkp = len(key_pieces)
    stride = 3 * nkp + 3
    assert tk <= 256 * 256 and A_DH + 2 * stride <= LANES

    @pl.when(i == 0)
    def _stage():
        def key_aug(t, e):
            blk = lax.shift_right_logical(lax.broadcasted_iota(jnp.int32, (tk, LANES), 0) + t * tk, SLC_SHIFT)
            aug = jnp.where(blk == lane, 1.0, 0.0)
            base = A_DH + stride * e
            for kp, piece in enumerate(key_pieces):
                aug = jnp.where((lane >= base + 3 * kp) & (lane < base + 3 * kp + 3), piece, aug)
            aug = jnp.where((lane >= base + 3 * nkp) & (lane < base + stride), -1.0, aug)
            return aug.astype(BF16)

        _stage_kv(kv_ref, ka_ref, vt_ref, 0, tk, key_aug)
        for p in range(2):
            ex = jnp.zeros((tq, LANES), F32)
            for e in range(2):
                base = A_DH + stride * e
                slope = jnp.broadcast_to(sl_ref[2 * p + e:2 * p + e + 1, 0:1], (tq, LANES))
                for j, piece in enumerate(_bf16_pieces(slope)):
                    for kp in range(nkp):
                        ex = jnp.where(lane == base + 3 * kp + j, piece, ex)
                for j, piece in enumerate(_bf16_pieces(slope * pos)):
                    ex = jnp.where(lane == base + 3 * nkp + j, piece, ex)
            ex_ref[p] = ex.astype(BF16)
        kk = lax.broadcasted_iota(jnp.int32, (tk, tq), 0)
        qq = lax.broadcasted_iota(jnp.int32, (tk, tq), 1)
        bd_ref[...] = jnp.where(kk <= qq, 0.0, NEG)

    sb = sb_ref[...]
    for p in range(2):
        qa_ref[p, :, 0:LANES] = q_ref[:, p * LANES:(p + 1) * LANES]
        qa_ref[p, :, LANES:2 * LANES] = jnp.where(lane < A_DH, sb, ex_ref[p])
    m_ref[...] = jnp.full(m_ref.shape, NEG, F32)
    acc_ref[...] = jnp.zeros(acc_ref.shape, F32)

    def scores(kt, s_ref):
        k0 = pl.multiple_of(kt * tk, tk)
        for hg in range(A_HG):
            s_ref[hg] = _dot_nt(ka_ref[hg % 2, pl.ds(k0, tk), :], qa_ref[hg // 2])

    def softmax_pv(kt, s_ref, causal=False):
        off = ((i - kt) * tk).astype(F32)
        vt = vt_ref[kt]
        m_all = m_ref[...]
        probs, alphas, m_rows = [], [], []
        for hg in range(A_HG):
            cst = -sl_ref[hg:hg + 1, 0:1] * off
            s = s_ref[hg] + bd_ref[...] if causal else s_ref[hg]
            m_prev = m_all[hg:hg + 1, :]
            m_new = jnp.maximum(m_prev, jnp.max(s, axis=0, keepdims=True) + cst)
            alphas.append(jnp.exp2(m_prev - m_new))
            probs.append(jnp.exp2(s - (m_new - cst)).astype(BF16))
            m_rows.append(m_new)
        m_ref[...] = jnp.concatenate(m_rows + [jnp.zeros((SUBLANES - A_HG, tq), F32)], axis=0)
        for hg in range(A_HG):
            acc_ref[hg] = acc_ref[hg] * alphas[hg] + _dot(vt, probs[hg])

    scores(0, sa_ref)

    def body(j, carry):
        kt = 2 * j
        scores(kt + 1, sb2_ref)
        softmax_pv(kt, sa_ref)
        scores(kt + 2, sa_ref)
        softmax_pv(kt + 1, sb2_ref)
        return carry

    lax.fori_loop(0, i // 2, body, 0)

    @pl.when(i % 2 == 0)
    def _():
        softmax_pv(i, sa_ref, causal=True)

    @pl.when(i % 2 == 1)
    def _():
        scores(i, sb2_ref)
        softmax_pv(i - 1, sa_ref)
        softmax_pv(i, sb2_ref, causal=True)

    gate_t = jax.nn.sigmoid(gl_ref[...]).T
    outs = []
    for hg in range(A_HG):
        w = gate_t[A_HG + hg:A_HG + hg + 1, :] / jnp.maximum(acc_ref[hg, A_DH:A_DH + 1, :], 1e-30)
        outs.append(acc_ref[hg, 0:A_DH, :] * w)
    o_ref[...] = jnp.concatenate(outs, axis=0).T.astype(o_ref.dtype)


def _slc_attn(a0, sb, gl, slopes, bsz, seq, kv_col):
    tq = TQ_SLC
    nq = seq // tq
    return pl.pallas_call(
        _slc_kernel,
        grid=(bsz, A_GROUPS, nq),
        in_specs=[
            pl.BlockSpec((tq, 2 * LANES), lambda b, g, i: (b * nq + i, g)),
            pl.BlockSpec((None, None, tq, LANES), lambda b, g, i: (b, g, i, 0)),
            pl.BlockSpec((seq, LANES), lambda b, g, i: (b, kv_col + g)),
            pl.BlockSpec((tq, LANES), lambda b, g, i: (b * nq + i, g)),
            pl.BlockSpec((None, SUBLANES, LANES), lambda b, g, i: (g, 0, 0)),
        ],
        out_specs=pl.BlockSpec((tq, 2 * LANES), lambda b, g, i: (b * nq + i, g)),
        out_shape=jax.ShapeDtypeStruct((bsz * seq, A_WIDTH), BF16),
        scratch_shapes=[
            pltpu.VMEM((2, seq, 2 * LANES), BF16),
            pltpu.VMEM((seq // tq, VT_ROWS, tq), BF16),
            pltpu.VMEM((2, tq, LANES), BF16),
            pltpu.VMEM((tq, tq), F32),
            pltpu.VMEM((SUBLANES, tq), F32),
            pltpu.VMEM((A_HG, VT_ROWS, tq), F32),
            pltpu.VMEM((2, tq, 2 * LANES), BF16),
            pltpu.VMEM((A_HG, tq, tq), F32),
            pltpu.VMEM((A_HG, tq, tq), F32),
        ],
        compiler_params=_cparams(("parallel", "parallel", "arbitrary")),
        name="selected_attention",
    )(a0, sb, a0, gl, slopes)


def _win_kernel(q_ref, kv_ref, gl_ref, sl_ref, o_ref, ka_ref, vt_ref, bw_ref):
    tq = TQ
    nsub = q_ref.shape[0] // tq
    wk = WIN + tq
    npad = WIN // tq
    i = pl.program_id(2)

    @pl.when(i == 0)
    def _stage():
        lane_p = lax.broadcasted_iota(jnp.int32, (WIN, LANES), 1)
        for e in range(2):
            ka_ref[e, 0:WIN, 0:LANES] = jnp.zeros((WIN, LANES), BF16)
            ka_ref[e, 0:WIN, LANES:2 * LANES] = jnp.where(lane_p == 0, 1.0, 0.0).astype(BF16)
        for t in range(npad):
            vt_ref[t] = jnp.zeros(vt_ref.shape[1:], BF16)
        _stage_kv(kv_ref, ka_ref, vt_ref, WIN, tq, lambda t, e: jnp.zeros((tq, LANES), BF16))
        kk = lax.broadcasted_iota(jnp.int32, (wk, tq), 0)
        qq = lax.broadcasted_iota(jnp.int32, (wk, tq), 1)
        dist = qq + WIN - kk
        ok = (dist >= 0) & (dist < WIN)
        distf = dist.astype(F32)
        for hg in range(A_HG):
            bw_ref[hg] = jnp.where(ok, -sl_ref[hg:hg + 1, 0:1] * distf, NEG)

    lane = lax.broadcasted_iota(jnp.int32, (tq, LANES), 1)
    negrow = jnp.where(lane == 0, NEG, 0.0).astype(BF16)
    gate_t = jax.nn.sigmoid(gl_ref[...]).T
    units = [(sub, hg) for sub in range(nsub) for hg in range(A_HG)]
    scores = []
    for sub in range(nsub):
        k0 = pl.multiple_of((i * nsub + sub) * tq, tq)
        rows = slice(sub * tq, (sub + 1) * tq)
        qa = [jnp.concatenate([q_ref[rows, p * LANES:(p + 1) * LANES], negrow], axis=1) for p in range(2)]
        scores += [_dot_nt(ka_ref[hg % 2, pl.ds(k0, wk), :], qa[hg // 2]) for hg in range(A_HG)]
    probs = []
    for u, (sub, hg) in enumerate(units):
        s = scores[u] + bw_ref[hg]
        probs.append(jnp.exp2(s - jnp.max(s, axis=0, keepdims=True)).astype(BF16))
    outs = []
    for u, (sub, hg) in enumerate(units):
        t0 = i * nsub + sub
        o = _dot(vt_ref[t0], probs[u][0:tq, :])
        for j in range(1, wk // tq):
            o = o + _dot(vt_ref[t0 + j], probs[u][j * tq:(j + 1) * tq, :])
        gate = gate_t[2 * A_HG + hg:2 * A_HG + hg + 1, sub * tq:(sub + 1) * tq]
        outs.append(o[0:A_DH, :] * (gate / jnp.maximum(o[A_DH:A_DH + 1, :], 1e-30)))
    for sub in range(nsub):
        tile = jnp.concatenate(outs[sub * A_HG:(sub + 1) * A_HG], axis=0).T
        o_ref[sub * tq:(sub + 1) * tq, :] = tile.astype(o_ref.dtype)


def _win_attn(a0, gl, slopes, bsz, seq, kv_col):
    tb = WIN_SUBTILES * TQ
    nq = seq // tb
    return pl.pallas_call(
        _win_kernel,
        grid=(bsz, A_GROUPS, nq),
        in_specs=[
            pl.BlockSpec((tb, 2 * LANES), lambda b, g, i: (b * nq + i, g)),
            pl.BlockSpec((seq, LANES), lambda b, g, i: (b, kv_col + g)),
            pl.BlockSpec((tb, LANES), lambda b, g, i: (b * nq + i, g)),
            pl.BlockSpec((None, SUBLANES, LANES), lambda b, g, i: (g, 0, 0)),
        ],
        out_specs=pl.BlockSpec((tb, 2 * LANES), lambda b, g, i: (b * nq + i, g)),
        out_shape=jax.ShapeDtypeStruct((bsz * seq, A_WIDTH), BF16),
        scratch_shapes=[
            pltpu.VMEM((2, WIN + seq, 2 * LANES), BF16),
            pltpu.VMEM(((WIN + seq) // TQ, VT_ROWS, TQ), BF16),
            pltpu.VMEM((A_HG, WIN + TQ, TQ), F32),
        ],
        compiler_params=_cparams(("parallel", "parallel", "arbitrary")),
        name="window_attention",
    )(a0, a0, gl, slopes)


def _outproj0_kernel(oc_ref, os_ref, ow_ref, z_ref, x_ref, gate_ref, w_ref, o_ref):
    o = oc_ref[...].astype(F32) + os_ref[...].astype(F32) + ow_ref[...].astype(F32)
    y = (o * _silu(z_ref[...].astype(F32))).astype(BF16)
    o_ref[...] = x_ref[...] + gate_ref[...] * _dot(y, w_ref[...])


def _outproj0(oc, osl, ow, a0, x2d, gate, w, seq, z_col):
    m, d = x2d.shape
    tm = TM_OUT
    per_b = seq // tm
    row = lambda i: (i, 0)
    return pl.pallas_call(
        _outproj0_kernel,
        grid=(m // tm,),
        in_specs=[
            pl.BlockSpec((tm, A_WIDTH), row),
            pl.BlockSpec((tm, A_WIDTH), row),
            pl.BlockSpec((tm, A_WIDTH), row),
            pl.BlockSpec((tm, A_WIDTH), lambda i: (i, z_col)),
            pl.BlockSpec((tm, d), row),
            pl.BlockSpec((None, 1, d), lambda i: (i // per_b, 0, 0)),
            pl.BlockSpec(w.shape, lambda i: (0, 0)),
        ],
        out_specs=pl.BlockSpec((tm, d), row),
        out_shape=jax.ShapeDtypeStruct((m, d), F32),
        compiler_params=_cparams(("parallel",)),
        name="nsa_out_proj",
    )(oc, osl, ow, a0, x2d, gate[:, None, :], w)


def _conv_kernel(x_ref, halo_ref, w_ref, b_ref, ks_ref, o_ref, *, per_b):
    first = (pl.program_id(0) % per_b) == 0
    halo = jnp.where(first, 0.0, halo_ref[...].astype(F32))
    xe = jnp.concatenate([halo, x_ref[...].astype(F32)], axis=0)
    y = xe[SUBLANES:, :] * w_ref[CONV_W - 1:CONV_W, :] + b_ref[...]
    for s in range(1, CONV_W):
        y = y + pltpu.roll(xe, s, 0)[SUBLANES:, :] * w_ref[CONV_W - 1 - s:CONV_W - s, :]
    o_ref[...] = (_silu(y) * ks_ref[...]).astype(o_ref.dtype)


def _conv_silu(a1, conv_w, conv_b, kscale, seq):
    m = a1.shape[0]
    tm, tc = TM_CONV, TC_CONV
    per_b = seq // tm
    hb = tm // SUBLANES
    kern = functools.partial(_conv_kernel, per_b=per_b)
    return pl.pallas_call(
        kern,
        grid=(m // tm, B_QK // tc),
        in_specs=[
            pl.BlockSpec((tm, tc), lambda i, j: (i, j)),
            pl.BlockSpec((SUBLANES, tc), lambda i, j: (jnp.maximum(i * hb - 1, 0), j)),
            pl.BlockSpec((CONV_W, tc), lambda i, j: (0, j)),
            pl.BlockSpec((1, tc), lambda i, j: (0, j)),
            pl.BlockSpec((1, tc), lambda i, j: (0, j)),
        ],
        out_specs=pl.BlockSpec((tm, tc), lambda i, j: (i, j)),
        out_shape=jax.ShapeDtypeStruct((m, B_QK), BF16),
        compiler_params=_cparams(("parallel", "parallel")),
        name="causal_conv_silu",
    )(a1, a1, conv_w, conv_b.reshape(1, B_QK), kscale)


def _mlstm_kernel(q_ref, k_ref, v_ref, g_ref, gb_ref, hg_ref, o_ref, c_ref, n_ref, m_ref):
    ln = q_ref.shape[0]
    nh = B_HEADS

    @pl.when(pl.program_id(1) == 0)
    def _():
        c_ref[...] = jnp.zeros(c_ref.shape, F32)
        n_ref[...] = jnp.zeros(n_ref.shape, F32)
        m_ref[...] = jnp.zeros(m_ref.shape, F32)

    lane = lax.broadcasted_iota(jnp.int32, (ln, LANES), 1)
    gi = (g_ref[:, 0:LANES] + gb_ref[:, 0:LANES]) * LOG2E
    gf = g_ref[:, LANES:2 * LANES] + gb_ref[:, LANES:2 * LANES]
    lf = (jnp.minimum(gf, 0.0) - jnp.log1p(jnp.exp(-jnp.abs(gf)))) * LOG2E
    r = lax.broadcasted_iota(jnp.int32, (ln, ln), 0)
    c = lax.broadcasted_iota(jnp.int32, (ln, ln), 1)
    causal = c <= r
    bcum = _dot(jnp.where(causal, 1.0, 0.0), lf, precision=HIGHEST)
    rel_t = (gi - bcum).T
    pm = rel_t[0:SUBLANES, :]
    lane_r = lax.broadcasted_iota(jnp.int32, pm.shape, 1)
    step = 1
    while step < ln:
        pm = jnp.maximum(pm, jnp.where(lane_r >= step, pltpu.roll(pm, step, 1), NEG))
        step *= 2
    pmax = jnp.concatenate([pm, jnp.zeros((LANES - SUBLANES, ln), F32)], axis=0).T
    m_st = m_ref[...]
    g_tot = bcum[ln - 1:ln, :]
    m_inter = bcum + m_st
    m_j = jnp.maximum(bcum + pmax, m_inter)
    e_arg = bcum - m_j
    w_inter = jnp.exp2(m_inter - m_j)
    a_all = g_tot - bcum + gi
    m_loc = jnp.max(a_all, axis=0, keepdims=True)
    w_all = jnp.exp2(a_all - m_loc)
    m_new = jnp.maximum(g_tot + m_st, m_loc)
    s_old = jnp.exp2(g_tot + m_st - m_new)
    s_new = jnp.exp2(m_loc - m_new)
    m_ref[...] = m_new

    def head(ref, h, width):
        return ref[:, h * width:(h + 1) * width]

    def onehot_col(h):
        return jnp.where(lane == h, 1.0, 0.0).astype(BF16)

    sqk = [_dot_nt(head(q_ref, h, B_DK), head(k_ref, h, B_DK)) for h in range(nh)]
    qc = [_dot(head(q_ref, h, B_DK), c_ref[h].astype(BF16)) for h in range(nh)]
    qn = _dot(head(q_ref, 0, B_DK), n_ref[0].astype(BF16))
    for h in range(1, nh):
        qn = qn + _dot(head(q_ref, h, B_DK), n_ref[h].astype(BF16))

    smats, kwts = [], []
    rowsum = jnp.zeros((ln, LANES), F32)
    for h in range(nh):
        log_w = jnp.where(causal, e_arg[:, h:h + 1] + rel_t[h:h + 1, :], NEG)
        smat = sqk[h] * jnp.exp2(log_w)
        rowsum = jnp.where(lane == h, jnp.sum(smat, axis=1, keepdims=True), rowsum)
        smats.append(smat.astype(BF16))
        kw = head(k_ref, h, B_DK).astype(F32) * w_all[:, h:h + 1]
        kwts.append(kw.T.astype(BF16))

    intra = [_dot(smats[h], head(v_ref, h, B_DV)) for h in range(nh)]
    c_loc = [_dot(kwts[h], head(v_ref, h, B_DV)) for h in range(nh)]
    n_loc = [_dot(kwts[h], onehot_col(h)) for h in range(nh)]

    us = []
    msq = jnp.zeros((ln, LANES), F32)
    for h in range(nh):
        c_ref[h] = s_old[:, h:h + 1] * c_ref[h] + s_new[:, h:h + 1] * c_loc[h]
        n_ref[h] = s_old[:, h:h + 1] * n_ref[h] + s_new[:, h:h + 1] * n_loc[h]
        u = intra[h] + w_inter[:, h:h + 1] * qc[h]
        msq = jnp.where(lane == h, jnp.mean(u * u, axis=-1, keepdims=True), msq)
        us.append(u)
    den = rowsum + w_inter * qn
    scale = 1.0 / jnp.maximum(jnp.abs(den), jnp.exp2(-m_j))
    f = scale * lax.rsqrt(scale * scale * msq + EPS)
    for h in range(nh):
        o_ref[:, h * B_DV:(h + 1) * B_DV] = (us[h] * f[:, h:h + 1] * head(hg_ref, h, B_DV)).astype(o_ref.dtype)


def _mlstm(qk, a1, g1, gbias, head_g, bsz, seq):
    ln = ML_CHUNK
    nc = seq // ln
    wq = B_QK // 2
    return pl.pallas_call(
        _mlstm_kernel,
        grid=(bsz, nc),
        in_specs=[
            pl.BlockSpec((ln, wq), lambda b, n: (b * nc + n, 0)),
            pl.BlockSpec((ln, wq), lambda b, n: (b * nc + n, 1)),
            pl.BlockSpec((ln, B_WIDTH), lambda b, n: (b * nc + n, B_QK // B_WIDTH)),
            pl.BlockSpec((ln, 2 * LANES), lambda b, n: (b * nc + n, 0)),
            pl.BlockSpec((1, 2 * LANES), lambda b, n: (0, 0)),
            pl.BlockSpec((1, B_WIDTH), lambda b, n: (0, 0)),
        ],
        out_specs=pl.BlockSpec((ln, B_WIDTH), lambda b, n: (b * nc + n, 0)),
        out_shape=jax.ShapeDtypeStruct((bsz * seq, B_WIDTH), BF16),
        scratch_shapes=[
            pltpu.VMEM((B_HEADS, B_DK, B_DV), F32),
            pltpu.VMEM((B_HEADS, B_DK, LANES), F32),
            pltpu.VMEM((1, LANES), F32),
        ],
        compiler_params=_cparams(("parallel", "arbitrary")),
        name="mlstm_chunkwise",
    )(qk, qk, a1, g1, gbias, head_g.reshape(1, B_WIDTH))


def _outproj1_kernel(h_ref, og_ref, z_ref, x_ref, gate_ref, w_ref, fg_ref, o_ref):
    y = jax.nn.sigmoid(og_ref[...].astype(F32)) * h_ref[...].astype(F32) * _silu(z_ref[...].astype(F32))
    x2 = x_ref[...] + gate_ref[...] * _dot(y.astype(BF16), w_ref[...])
    o_ref[...] = x2 * lax.rsqrt(jnp.mean(x2 * x2, axis=-1, keepdims=True) + EPS) * fg_ref[...]


def _outproj1(hn, a1, x2d, gate, w, final_g, seq):
    m, d = x2d.shape
    tm = TM_OUT
    per_b = seq // tm
    row = lambda i: (i, 0)
    og_col = B_QK // B_WIDTH + 1
    return pl.pallas_call(
        _outproj1_kernel,
        grid=(m // tm,),
        in_specs=[
            pl.BlockSpec((tm, B_WIDTH), row),
            pl.BlockSpec((tm, B_WIDTH), lambda i: (i, og_col)),
            pl.BlockSpec((tm, B_WIDTH), lambda i: (i, og_col + 1)),
            pl.BlockSpec((tm, d), row),
            pl.BlockSpec((None, 1, d), lambda i: (i // per_b, 0, 0)),
            pl.BlockSpec(w.shape, lambda i: (0, 0)),
            pl.BlockSpec((1, d), lambda i: (0, 0)),
        ],
        out_specs=pl.BlockSpec((tm, d), row),
        out_shape=jax.ShapeDtypeStruct((m, d), F32),
        compiler_params=_cparams(("parallel",)),
        name="mlstm_out_proj_final_norm",
    )(hn, a1, a1, x2d, gate[:, None, :], w, final_g.reshape(1, d))


def _alibi_slopes():
    return np.asarray(2.0 ** (-8.0 * np.arange(1, A_HEADS + 1) / A_HEADS), np.float32)


def _selection_overlap_t(seq, cols):
    n_cmp = seq // CMP_STRIDE - CMP_LEN // CMP_STRIDE + 1
    n_sel = seq // SLC_LEN
    c0 = np.arange(n_cmp)[None, :] * CMP_STRIDE
    s0 = np.arange(n_sel)[:, None] * SLC_LEN
    ov = np.clip(np.minimum(c0 + CMP_LEN, s0 + SLC_LEN) - np.maximum(c0, s0), 0, None) / CMP_LEN
    out = np.zeros((n_sel, cols), np.float32)
    out[:, :n_cmp] = ov
    return out


def _layer0_weights(w_in):
    d = w_in.shape[0]
    kv0 = A_WIDTH
    gl0 = kv0 + 6 * A_GROUPS * A_DH
    z0 = gl0 + 3 * A_HEADS
    wq = w_in[:, :A_WIDTH] * (A_DH ** -0.5 * LOG2E)
    wz = w_in[:, z0:z0 + A_WIDTH]
    wkv = w_in[:, kv0:gl0].reshape(d, 6, A_GROUPS, A_DH)
    branches = [wkv[:, 2 * r:2 * r + 2].transpose(0, 2, 1, 3).reshape(d, A_GROUPS * 2 * A_DH) for r in range(3)]
    w0 = jnp.concatenate([wq, wz] + branches, axis=1).astype(BF16)
    wgl = w_in[:, gl0:z0].reshape(d, 3, A_GROUPS, A_HG).transpose(0, 2, 1, 3).reshape(d, A_GROUPS, 3 * A_HG)
    wgl = jnp.pad(wgl, ((0, 0), (0, 0), (0, LANES - 3 * A_HG))).reshape(d, A_GROUPS * LANES).astype(BF16)
    return w0, wgl


def _compress_weights(pe, w1, b1, w2, b2):
    half = CMP_LEN // 2
    w1r = w1.reshape(2, 2, half, A_DH, CMP_HIDDEN)
    wc = jnp.einsum("khldn,kq->lkdqhn", w1r, jnp.eye(2, dtype=w1.dtype))
    wc = wc.reshape(half * 2 * A_DH, 4 * CMP_HIDDEN).astype(BF16)
    pec = pe.reshape(2, 2, half, A_DH).transpose(1, 2, 0, 3).reshape(2, half * 2 * A_DH)
    pec = jnp.pad(pec, ((0, SUBLANES - 2), (0, 0))).astype(BF16)
    b1c = b1.reshape(1, 2 * CMP_HIDDEN)
    w2k = jnp.zeros((CMP_HIDDEN, 2 * LANES), F32)
    w2k = w2k.at[:, 0:A_DH].set(w2[0]).at[:, LANES + A_DH:2 * LANES].set(w2[0])
    b2k = jnp.zeros((1, 2 * LANES), F32)
    b2k = b2k.at[0, 0:A_DH].set(b2[0]).at[0, LANES + A_DH:2 * LANES].set(b2[0])
    w2vt = w2[1].T.astype(BF16)
    b2v = jnp.broadcast_to(b2[1][:, None], (A_DH, LANES))
    return wc, pec, b1c, w2k.astype(BF16), b2k, w2vt, b2v


def _layer1_weights(w_in):
    g0 = B_QK + B_WIDTH
    w_main = jnp.concatenate([w_in[:, :g0], w_in[:, g0 + 2 * B_HEADS:]], axis=1).astype(BF16)
    pad = ((0, 0), (0, LANES - B_HEADS))
    wg = jnp.concatenate([jnp.pad(w_in[:, g0:g0 + B_HEADS], pad),
                          jnp.pad(w_in[:, g0 + B_HEADS:g0 + 2 * B_HEADS], pad)], axis=1).astype(BF16)
    return w_main, wg


def kernel(x, c, ada_w, ada_b, norm_g, final_g, a_w_in, a_cmp_pe, a_cmp_w1, a_cmp_b1, a_cmp_w2, a_cmp_b2,
           a_w_out, b_w_in, b_conv_w, b_conv_b, b_gate_b, b_head_g, b_w_out):
    bsz, seq, d = x.shape
    assert d == D_MODEL and seq % (2 * TM_PROJ) == 0 and ada_w.shape[0] == 2
    m = bsz * seq
    x2d = x.reshape(m, d)
    mod = _ada_mod(c, ada_w, ada_b)

    w0, wgl = _layer0_weights(a_w_in[0])
    a0, gl = _normmod_matmul(x2d, norm_g[0], mod[0, :, d:2 * d], mod[0, :, :d], w0, wgl, seq, TN_PROJ0, "nsa_in_proj")
    z_col = 1
    kv_col = 2 * A_WIDTH // LANES
    n_chunk = seq // CMP_STRIDE
    n_cmp = n_chunk - CMP_LEN // CMP_STRIDE + 1
    n_sel = seq // SLC_LEN
    n_top = min(SLC_TOP, n_sel)
    assert n_sel % SUBLANES == 0 and n_sel <= LANES
    kvsrc = a0[:, 2 * A_WIDTH:2 * A_WIDTH + A_GROUPS * LANES]
    xc = kvsrc.reshape(bsz, n_chunk, CMP_STRIDE, A_GROUPS, LANES).transpose(0, 3, 1, 2, 4)
    xc = xc.reshape(bsz, A_GROUPS, n_chunk, CMP_STRIDE * LANES)
    kc, vct = _compress(xc, *_compress_weights(a_cmp_pe[0], a_cmp_w1[0], a_cmp_b1[0], a_cmp_w2[0], a_cmp_b2[0]))
    slopes = jnp.asarray(np.broadcast_to(
        np.pad((_alibi_slopes() * np.float32(LOG2E)).reshape(A_GROUPS, A_HG), ((0, 0), (0, SUBLANES - A_HG)))[:, :, None],
        (A_GROUPS, SUBLANES, LANES)))
    ovt = jnp.asarray(_selection_overlap_t(seq, n_chunk))
    o_cmp, sb = _cmp_attn(a0, kc, vct, gl, slopes, ovt, bsz, seq, n_cmp, n_sel, n_top)
    o_slc = _slc_attn(a0, sb, gl, slopes, bsz, seq, kv_col + A_GROUPS)
    o_win = _win_attn(a0, gl, slopes, bsz, seq, kv_col + 2 * A_GROUPS)
    x1 = _outproj0(o_cmp, o_slc, o_win, a0, x2d, mod[0, :, 2 * d:], a_w_out[0].astype(BF16), seq, z_col)

    w1m, w1g = _layer1_weights(b_w_in[0])
    a1, g1 = _normmod_matmul(x1, norm_g[1], mod[1, :, d:2 * d], mod[1, :, :d], w1m, w1g, seq, TN_PROJ1, "mlstm_in_proj")
    kscale = jnp.concatenate([jnp.ones((1, B_QK // 2), F32), jnp.full((1, B_QK // 2), B_DK ** -0.5, F32)], axis=1)
    qk = _conv_silu(a1, b_conv_w[0], b_conv_b[0], kscale, seq)
    gbias = jnp.pad(b_gate_b[0], ((0, 0), (0, LANES - B_HEADS))).reshape(1, 2 * LANES)
    y1 = _mlstm(qk, a1, g1, gbias, b_head_g[0], bsz, seq)
    out = _outproj1(y1, a1, x1, mod[1, :, 2 * d:], b_w_out[0].astype(BF16), final_g, seq)
    return out.reshape(bsz, seq, d)
```

```python
import functools

import numpy as np
import jax
import jax.numpy as jnp
from jax import lax
from jax.experimental import pallas as pl
from jax.experimental.pallas import tpu as pltpu

F32 = jnp.float32
BF16 = jnp.bfloat16
HIGHEST = lax.Precision.HIGHEST

EPS = 1e-6
NEG = -1e30
LOG2E = float(np.log2(np.e))

D_MODEL = 1024
A_HEADS = 16
A_GROUPS = 4
A_HG = A_HEADS // A_GROUPS
A_DH = 64
A_WIDTH = A_HEADS * A_DH
CMP_LEN = 32
CMP_STRIDE = 16
CMP_HIDDEN = 256
SLC_LEN = 64
SLC_SHIFT = 6
SLC_TOP = 16
WIN = 512
B_HEADS = 8
B_DK = 128
B_DV = 256
B_WIDTH = B_HEADS * B_DV
B_QK = 2 * B_HEADS * B_DK
CONV_W = 4

LANES = 128
SUBLANES = 8
VMEM_LIMIT = 56 * 1024 * 1024

TQ = 256
TQ_SLC = 512
TQ_CMP = 1024
WIN_SUBTILES = 2
VT_ROWS = A_DH + 16
ML_CHUNK = 512
TM_PROJ = 1024
TN_PROJ0 = 1792
TN_PROJ1 = 2048
TM_OUT = 512
TM_CONV = 1024
TC_CONV = 512

assert WIN % TQ == 0 and 1 << SLC_SHIFT == SLC_LEN


def _dot(a, b, **kw):
    return jnp.dot(a, b, preferred_element_type=F32, **kw)


def _dot_nt(a, b):
    return lax.dot_general(a, b, (((1,), (1,)), ((), ())), preferred_element_type=F32)


def _cparams(sem):
    return pltpu.CompilerParams(dimension_semantics=sem, vmem_limit_bytes=VMEM_LIMIT)


def _silu(x):
    return x * jax.nn.sigmoid(x)


def _ada_kernel(c_ref, w_ref, b_ref, o_ref):
    o_ref[...] = _dot(c_ref[...], w_ref[...], precision=HIGHEST) + b_ref[...]


def _ada_mod(c, ada_w, ada_b):
    depth, d, d3 = ada_w.shape
    bsz = c.shape[0]
    nj = d3 // d
    return pl.pallas_call(
        _ada_kernel,
        grid=(depth, nj),
        in_specs=[
            pl.BlockSpec((bsz, d), lambda i, j: (0, 0)),
            pl.BlockSpec((None, d, d), lambda i, j: (i, 0, j)),
            pl.BlockSpec((None, 1, d), lambda i, j: (i, 0, j)),
        ],
        out_specs=pl.BlockSpec((None, bsz, d), lambda i, j: (i, 0, j)),
        out_shape=jax.ShapeDtypeStruct((depth, bsz, d3), F32),
        compiler_params=_cparams(("parallel", "parallel")),
        name="ada_mod",
    )(c, ada_w, ada_b.reshape(depth, 1, d3))


def _normmod_kernel(x_ref, g_ref, sc_ref, sh_ref, w_ref, wg_ref, o_ref, og_ref, h_ref):
    @pl.when(pl.program_id(1) == 0)
    def _():
        x = x_ref[...]
        xn = x * lax.rsqrt(jnp.mean(x * x, axis=-1, keepdims=True) + EPS)
        h = (xn * g_ref[...]) * (1.0 + sc_ref[...]) + sh_ref[...]
        hb = h.astype(BF16)
        h_ref[...] = hb
        half = hb.shape[0] // 2
        og_ref[0:half, :] = _dot(hb[0:half, :], wg_ref[...])
        og_ref[half:, :] = _dot(hb[half:, :], wg_ref[...])

    o_ref[...] = _dot(h_ref[...], w_ref[...]).astype(o_ref.dtype)


def _normmod_matmul(x2d, g, scale, shift, w, wg, seq, tn, name):
    m, d = x2d.shape
    n = w.shape[1]
    ng = wg.shape[1]
    tm = TM_PROJ
    assert m % tm == 0 and n % tn == 0 and seq % tm == 0
    per_b = seq // tm
    w_tiles = w.reshape(d, n // tn, tn).transpose(1, 0, 2)
    return pl.pallas_call(
        _normmod_kernel,
        grid=(m // tm, n // tn),
        in_specs=[
            pl.BlockSpec((tm, d), lambda i, j: (i, 0)),
            pl.BlockSpec((1, d), lambda i, j: (0, 0)),
            pl.BlockSpec((None, 1, d), lambda i, j: (i // per_b, 0, 0)),
            pl.BlockSpec((None, 1, d), lambda i, j: (i // per_b, 0, 0)),
            pl.BlockSpec((None, d, tn), lambda i, j: (j, 0, 0)),
            pl.BlockSpec((d, ng), lambda i, j: (0, 0)),
        ],
        out_specs=[
            pl.BlockSpec((tm, tn), lambda i, j: (i, j)),
            pl.BlockSpec((tm, ng), lambda i, j: (i, 0)),
        ],
        out_shape=[jax.ShapeDtypeStruct((m, n), BF16), jax.ShapeDtypeStruct((m, ng), F32)],
        scratch_shapes=[pltpu.VMEM((tm, d), BF16)],
        compiler_params=_cparams(("parallel", "arbitrary")),
        name=name,
    )(x2d, g.reshape(1, d), scale[:, None, :], shift[:, None, :], w_tiles, wg)


def _compress_kernel(x_ref, wc_ref, pe_ref, b1_ref, w2k_ref, b2k_ref, w2vt_ref, b2v_ref, ok_ref, ov_ref):
    a = _dot(x_ref[...], wc_ref[...])
    pb = _dot(pe_ref[...], wc_ref[...])
    n = a.shape[0]
    hid = CMP_HIDDEN
    hs = []
    for kv in range(2):
        c0 = kv * 2 * hid
        first = a[:, c0:c0 + hid]
        second = pltpu.roll(a[:, c0 + hid:c0 + 2 * hid], n - 1, 0)
        bias = pb[0:1, c0:c0 + hid] + pb[1:2, c0 + hid:c0 + 2 * hid] + b1_ref[:, kv * hid:(kv + 1) * hid]
        hs.append(jax.nn.gelu(first + second + bias))
    ok_ref[...] = (_dot(hs[0].astype(BF16), w2k_ref[...]) + b2k_ref[...]).astype(ok_ref.dtype)
    ov_ref[...] = (_dot_nt(w2vt_ref[...], hs[1].astype(BF16)) + b2v_ref[:, 0:1]).astype(ov_ref.dtype)


def _compress(xc, wc, pec, b1c, w2k, b2k, w2vt, b2v):
    bsz, g, nchunk, kdim = xc.shape
    full = lambda a: pl.BlockSpec(a.shape, lambda b, gi: (0,) * a.ndim)
    return pl.pallas_call(
        _compress_kernel,
        grid=(bsz, g),
        in_specs=[pl.BlockSpec((None, None, nchunk, kdim), lambda b, gi: (b, gi, 0, 0)),
                  full(wc), full(pec), full(b1c), full(w2k), full(b2k), full(w2vt), full(b2v)],
        out_specs=[
            pl.BlockSpec((None, None, nchunk, 2 * LANES), lambda b, gi: (b, gi, 0, 0)),
            pl.BlockSpec((None, None, A_DH, nchunk), lambda b, gi: (b, gi, 0, 0)),
        ],
        out_shape=[
            jax.ShapeDtypeStruct((bsz, g, nchunk, 2 * LANES), BF16),
            jax.ShapeDtypeStruct((bsz, g, A_DH, nchunk), BF16),
        ],
        compiler_params=_cparams(("parallel", "parallel")),
        name="compress_tokens",
    )(xc, wc, pec, b1c, w2k, b2k, w2vt, b2v)


def _cmp_attn_kernel(q_ref, k_ref, vt_ref, gl_ref, sl_ref, ovt_ref, o_ref, sb_ref, *, n_cmp, n_sel, n_top):
    tq = q_ref.shape[0]
    ncol = k_ref.shape[0]
    q0 = pl.program_id(2) * tq
    tpos = lax.broadcasted_iota(jnp.int32, (ncol, tq), 1) + q0
    cidx = lax.broadcasted_iota(jnp.int32, (ncol, tq), 0)
    valid = (cidx * CMP_STRIDE + (CMP_LEN - 1) <= tpos) & (cidx < n_cmp)
    gate_t = jax.nn.sigmoid(gl_ref[...]).T
    vt = vt_ref[...]
    lane_k = lax.broadcasted_iota(jnp.int32, (ncol, LANES), 1)
    kpos = (lax.broadcasted_iota(jnp.int32, (ncol, LANES), 0) * CMP_STRIDE + (CMP_LEN - 1)).astype(F32)
    kpos_hi = jnp.floor(kpos * (1.0 / 256.0)) * 256.0
    lane_q = lax.broadcasted_iota(jnp.int32, (tq, LANES), 1)
    kaug, qaug = [], []
    for e in range(2):
        aug = jnp.where((lane_k >= 6 * e) & (lane_k < 6 * e + 3), kpos_hi, 0.0)
        aug = jnp.where((lane_k >= 6 * e + 3) & (lane_k < 6 * e + 6), kpos - kpos_hi, aug)
        kaug.append(aug.astype(BF16))
    for p in range(2):
        aug = jnp.zeros((tq, LANES), F32)
        for e in range(2):
            slope = jnp.broadcast_to(sl_ref[2 * p + e:2 * p + e + 1, 0:1], (tq, LANES))
            for j, piece in enumerate(_bf16_pieces(slope)):
                aug = jnp.where((lane_q == 6 * e + j) | (lane_q == 6 * e + 3 + j), piece, aug)
        qaug.append(aug.astype(BF16))
    scores = []
    for hg in range(A_HG):
        p, e = divmod(hg, 2)
        keys = jnp.concatenate([k_ref[:, e * LANES:(e + 1) * LANES], kaug[e]], axis=1)
        qrs = jnp.concatenate([q_ref[:, p * LANES:(p + 1) * LANES], qaug[p]], axis=1)
        scores.append(_dot_nt(keys, qrs))
    probs = []
    psum = jnp.zeros((ncol, tq), F32)
    for hg in range(A_HG):
        s = jnp.where(valid, scores[hg], NEG)
        m = jnp.max(s, axis=0, keepdims=True)
        m = jnp.where(m > 0.5 * NEG, m, 0.0)
        pr = jnp.exp2(s - m)
        pr = pr * (1.0 / jnp.maximum(jnp.sum(pr, axis=0, keepdims=True), 1e-30))
        psum = psum + pr
        probs.append(pr.astype(BF16))
    outs = [_dot(vt, probs[hg]) * gate_t[hg:hg + 1, :] for hg in range(A_HG)]
    o_ref[...] = jnp.concatenate(outs, axis=0).T.astype(o_ref.dtype)

    imp = _dot(ovt_ref[...], psum, precision=HIGHEST)
    blk = lax.broadcasted_iota(jnp.int32, (n_sel, tq), 0)
    blkf = blk.astype(F32)
    cur = lax.shift_right_logical(lax.broadcasted_iota(jnp.int32, (n_sel, tq), 1) + q0, SLC_SHIFT)
    forced = (blk == 0) | (blk == cur) | (blk == cur - 1)
    score = jnp.where(forced | (blk > cur), -1.0, imp)
    sel = jnp.where(forced, 1.0, 0.0)
    for _ in range(n_top - 3):
        mx = jnp.max(score, axis=0, keepdims=True)
        idx = jnp.min(jnp.where(score == mx, blkf, 1e9), axis=0, keepdims=True)
        pick = blkf == idx
        sel = jnp.where(pick, 1.0, sel)
        score = jnp.where(pick, -3.0, score)
    bias_t = jnp.where(sel > 0.0, 0.0, NEG)
    bias_t = jnp.concatenate([bias_t, jnp.zeros((LANES - n_sel, tq), F32)], axis=0)
    sb_ref[...] = bias_t.T.astype(sb_ref.dtype)


def _cmp_attn(a0, kc, vct, gl, slopes, ovt, bsz, seq, n_cmp, n_sel, n_top):
    tq = TQ_CMP
    nq = seq // tq
    ncol = kc.shape[2]
    kern = functools.partial(_cmp_attn_kernel, n_cmp=n_cmp, n_sel=n_sel, n_top=n_top)
    return pl.pallas_call(
        kern,
        grid=(bsz, A_GROUPS, nq),
        in_specs=[
            pl.BlockSpec((tq, 2 * LANES), lambda b, g, i: (b * nq + i, g)),
            pl.BlockSpec((None, None, ncol, 2 * LANES), lambda b, g, i: (b, g, 0, 0)),
            pl.BlockSpec((None, None, A_DH, ncol), lambda b, g, i: (b, g, 0, 0)),
            pl.BlockSpec((tq, LANES), lambda b, g, i: (b * nq + i, g)),
            pl.BlockSpec((None, SUBLANES, LANES), lambda b, g, i: (g, 0, 0)),
            pl.BlockSpec(ovt.shape, lambda b, g, i: (0, 0)),
        ],
        out_specs=[
            pl.BlockSpec((tq, 2 * LANES), lambda b, g, i: (b * nq + i, g)),
            pl.BlockSpec((None, None, tq, LANES), lambda b, g, i: (b, g, i, 0)),
        ],
        out_shape=[
            jax.ShapeDtypeStruct((bsz * seq, A_WIDTH), BF16),
            jax.ShapeDtypeStruct((bsz, A_GROUPS, seq, LANES), BF16),
        ],
        compiler_params=_cparams(("parallel", "parallel", "parallel")),
        name="cmp_attention_topk",
    )(a0, kc, vct, gl, slopes, ovt)


def _stage_kv(kv_ref, ka_ref, vt_ref, row0, chunk, aug_fn):
    seq = kv_ref.shape[0]
    for t in range(seq // chunk):
        kvf = kv_ref[t * chunk:(t + 1) * chunk, :].astype(F32)
        low = lax.broadcasted_iota(jnp.int32, kvf.shape, 1) < A_DH
        rolled = pltpu.roll(kvf, A_DH, 1)
        r0 = row0 + t * chunk
        ka_ref[0, r0:r0 + chunk, 0:LANES] = jnp.where(low, kvf, 0.0).astype(BF16)
        ka_ref[1, r0:r0 + chunk, 0:LANES] = jnp.where(low, 0.0, rolled).astype(BF16)
        ka_ref[0, r0:r0 + chunk, LANES:2 * LANES] = aug_fn(t, 0)
        ka_ref[1, r0:r0 + chunk, LANES:2 * LANES] = aug_fn(t, 1)
        vt_ref[row0 // chunk + t, 0:A_DH, :] = kvf.T[A_DH:, :].astype(BF16)
        ones_rows = lax.broadcasted_iota(jnp.int32, (VT_ROWS - A_DH, chunk), 0) == 0
        vt_ref[row0 // chunk + t, A_DH:VT_ROWS, :] = jnp.where(ones_rows, 1.0, 0.0).astype(BF16)


def _bf16_pieces(x):
    p1 = x.astype(BF16)
    r1 = x - p1.astype(F32)
    p2 = r1.astype(BF16)
    p3 = (r1 - p2.astype(F32)).astype(BF16)
    return [p1.astype(F32), p2.astype(F32), p3.astype(F32)]


def _slc_kernel(q_ref, sb_ref, kv_ref, gl_ref, sl_ref, o_ref,
                ka_ref, vt_ref, ex_ref, bd_ref, m_ref, acc_ref, qa_ref, sa_ref, sb2_ref):
    tq = q_ref.shape[0]
    tk = tq
    i = pl.program_id(2)

    lane = lax.broadcasted_iota(jnp.int32, (tq, LANES), 1)
    pos = lax.broadcasted_iota(jnp.int32, (tq, LANES), 0).astype(F32)
    pos_hi = jnp.floor(pos * (1.0 / 256.0)) * 256.0
    key_pieces = [pos] if tk <= 256 else [pos_hi, pos - pos_hi]
    nkp = len(key_pieces)
    stride = 3 * nkp + 3
    assert tk <= 256 * 256 and A_DH + 2 * stride <= LANES

    @pl.when(i == 0)
    def _stage():
        def key_aug(t, e):
            blk = lax.shift_right_logical(lax.broadcasted_iota(jnp.int32, (tk, LANES), 0) + t * tk, SLC_SHIFT)
            aug = jnp.where(blk == lane, 1.0, 0.0)
            base = A_DH + stride * e
            for kp, piece in enumerate(key_pieces):
                aug = jnp.where((lane >= base + 3 * kp) & (lane < base + 3 * kp + 3), piece, aug)
            aug = jnp.where((lane >= base + 3 * nkp) & (lane < base + stride), -1.0, aug)
            return aug.astype(BF16)

        _stage_kv(kv_ref, ka_ref, vt_ref, 0, tk, key_aug)
        for p in range(2):
            ex = jnp.zeros((tq, LANES), F32)
            for e in range(2):
                base = A_DH + stride * e
                slope = jnp.broadcast_to(sl_ref[2 * p + e:2 * p + e + 1, 0:1], (tq, LANES))
                for j, piece in enumerate(_bf16_pieces(slope)):
                    for kp in range(nkp):
                        ex = jnp.where(lane == base + 3 * kp + j, piece, ex)
                for j, piece in enumerate(_bf16_pieces(slope * pos)):
                    ex = jnp.where(lane == base + 3 * nkp + j, piece, ex)
            ex_ref[p] = ex.astype(BF16)
        kk = lax.broadcasted_iota(jnp.int32, (tk, tq), 0)
        qq = lax.broadcasted_iota(jnp.int32, (tk, tq), 1)
        bd_ref[...] = jnp.where(kk <= qq, 0.0, NEG)

    sb = sb_ref[...]
    for p in range(2):
        qa_ref[p, :, 0:LANES] = q_ref[:, p * LANES:(p + 1) * LANES]
        qa_ref[p, :, LANES:2 * LANES] = jnp.where(lane < A_DH, sb, ex_ref[p])
    m_ref[...] = jnp.full(m_ref.shape, NEG, F32)
    acc_ref[...] = jnp.zeros(acc_ref.shape, F32)

    def scores(kt, s_ref):
        k0 = pl.multiple_of(kt * tk, tk)
        for hg in range(A_HG):
            s_ref[hg] = _dot_nt(ka_ref[hg % 2, pl.ds(k0, tk), :], qa_ref[hg // 2])

    def softmax_pv(kt, s_ref, causal=False):
        off = ((i - kt) * tk).astype(F32)
        vt = vt_ref[kt]
        m_all = m_ref[...]
        probs, alphas, m_rows = [], [], []
        for hg in range(A_HG):
            cst = -sl_ref[hg:hg + 1, 0:1] * off
            s = s_ref[hg] + bd_ref[...] if causal else s_ref[hg]
            m_prev = m_all[hg:hg + 1, :]
            m_new = jnp.maximum(m_prev, jnp.max(s, axis=0, keepdims=True) + cst)
            alphas.append(jnp.exp2(m_prev - m_new))
            probs.append(jnp.exp2(s - (m_new - cst)).astype(BF16))
            m_rows.append(m_new)
        m_ref[...] = jnp.concatenate(m_rows + [jnp.zeros((SUBLANES - A_HG, tq), F32)], axis=0)
        for hg in range(A_HG):
            acc_ref[hg] = acc_ref[hg] * alphas[hg] + _dot(vt, probs[hg])

    scores(0, sa_ref)

    def body(j, carry):
        kt = 2 * j
        scores(kt + 1, sb2_ref)
        softmax_pv(kt, sa_ref)
        scores(kt + 2, sa_ref)
        softmax_pv(kt + 1, sb2_ref)
        return carry

    lax.fori_loop(0, i // 2, body, 0)

    @pl.when(i % 2 == 0)
    def _():
        softmax_pv(i, sa_ref, causal=True)

    @pl.when(i % 2 == 1)
    def _():
        scores(i, sb2_ref)
        softmax_pv(i - 1, sa_ref)
        softmax_pv(i, sb2_ref, causal=True)

    gate_t = jax.nn.sigmoid(gl_ref[...]).T
    outs = []
    for hg in range(A_HG):
        w = gate_t[A_HG + hg:A_HG + hg + 1, :] / jnp.maximum(acc_ref[hg, A_DH:A_DH + 1, :], 1e-30)
        outs.append(acc_ref[hg, 0:A_DH, :] * w)
    o_ref[...] = jnp.concatenate(outs, axis=0).T.astype(o_ref.dtype)


def _slc_attn(a0, sb, gl, slopes, bsz, seq, kv_col):
    tq = TQ_SLC
    nq = seq // tq
    return pl.pallas_call(
        _slc_kernel,
        grid=(bsz, A_GROUPS, nq),
        in_specs=[
            pl.BlockSpec((tq, 2 * LANES), lambda b, g, i: (b * nq + i, g)),
            pl.BlockSpec((None, None, tq, LANES), lambda b, g, i: (b, g, i, 0)),
            pl.BlockSpec((seq, LANES), lambda b, g, i: (b, kv_col + g)),
            pl.BlockSpec((tq, LANES), lambda b, g, i: (b * nq + i, g)),
            pl.BlockSpec((None, SUBLANES, LANES), lambda b, g, i: (g, 0, 0)),
        ],
        out_specs=pl.BlockSpec((tq, 2 * LANES), lambda b, g, i: (b * nq + i, g)),
        out_shape=jax.ShapeDtypeStruct((bsz * seq, A_WIDTH), BF16),
        scratch_shapes=[
            pltpu.VMEM((2, seq, 2 * LANES), BF16),
            pltpu.VMEM((seq // tq, VT_ROWS, tq), BF16),
            pltpu.VMEM((2, tq, LANES), BF16),
            pltpu.VMEM((tq, tq), F32),
            pltpu.VMEM((SUBLANES, tq), F32),
            pltpu.VMEM((A_HG, VT_ROWS, tq), F32),
            pltpu.VMEM((2, tq, 2 * LANES), BF16),
            pltpu.VMEM((A_HG, tq, tq), F32),
            pltpu.VMEM((A_HG, tq, tq), F32),
        ],
        compiler_params=_cparams(("parallel", "parallel", "arbitrary")),
        name="selected_attention",
    )(a0, sb, a0, gl, slopes)


def _win_kernel(q_ref, kv_ref, gl_ref, sl_ref, o_ref, ka_ref, vt_ref, bw_ref):
    tq = TQ
    nsub = q_ref.shape[0] // tq
    wk = WIN + tq
    npad = WIN // tq
    i = pl.program_id(2)

    @pl.when(i == 0)
    def _stage():
        lane_p = lax.broadcasted_iota(jnp.int32, (WIN, LANES), 1)
        for e in range(2):
            ka_ref[e, 0:WIN, 0:LANES] = jnp.zeros((WIN, LANES), BF16)
            ka_ref[e, 0:WIN, LANES:2 * LANES] = jnp.where(lane_p == 0, 1.0, 0.0).astype(BF16)
        for t in range(npad):
            vt_ref[t] = jnp.zeros(vt_ref.shape[1:], BF16)
        _stage_kv(kv_ref, ka_ref, vt_ref, WIN, tq, lambda t, e: jnp.zeros((tq, LANES), BF16))
        kk = lax.broadcasted_iota(jnp.int32, (wk, tq), 0)
        qq = lax.broadcasted_iota(jnp.int32, (wk, tq), 1)
        dist = qq + WIN - kk
        ok = (dist >= 0) & (dist < WIN)
        distf = dist.astype(F32)
        for hg in range(A_HG):
            bw_ref[hg] = jnp.where(ok, -sl_ref[hg:hg + 1, 0:1] * distf, NEG)

    lane = lax.broadcasted_iota(jnp.int32, (tq, LANES), 1)
    negrow = jnp.where(lane == 0, NEG, 0.0).astype(BF16)
    gate_t = jax.nn.sigmoid(gl_ref[...]).T
    units = [(sub, hg) for sub in range(nsub) for hg in range(A_HG)]
    scores = []
    for sub in range(nsub):
        k0 = pl.multiple_of((i * nsub + sub) * tq, tq)
        rows = slice(sub * tq, (sub + 1) * tq)
        qa = [jnp.concatenate([q_ref[rows, p * LANES:(p + 1) * LANES], negrow], axis=1) for p in range(2)]
        scores += [_dot_nt(ka_ref[hg % 2, pl.ds(k0, wk), :], qa[hg // 2]) for hg in range(A_HG)]
    probs = []
    for u, (sub, hg) in enumerate(units):
        s = scores[u] + bw_ref[hg]
        probs.append(jnp.exp2(s - jnp.max(s, axis=0, keepdims=True)).astype(BF16))
    outs = []
    for u, (sub, hg) in enumerate(units):
        t0 = i * nsub + sub
        o = _dot(vt_ref[t0], probs[u][0:tq, :])
        for j in range(1, wk // tq):
            o = o + _dot(vt_ref[t0 + j], probs[u][j * tq:(j + 1) * tq, :])
        gate = gate_t[2 * A_HG + hg:2 * A_HG + hg + 1, sub * tq:(sub + 1) * tq]
        outs.append(o[0:A_DH, :] * (gate / jnp.maximum(o[A_DH:A_DH + 1, :], 1e-30)))
    for sub in range(nsub):
        tile = jnp.concatenate(outs[sub * A_HG:(sub + 1) * A_HG], axis=0).T
        o_ref[sub * tq:(sub + 1) * tq, :] = tile.astype(o_ref.dtype)


def _win_attn(a0, gl, slopes, bsz, seq, kv_col):
    tb = WIN_SUBTILES * TQ
    nq = seq // tb
    return pl.pallas_call(
        _win_kernel,
        grid=(bsz, A_GROUPS, nq),
        in_specs=[
            pl.BlockSpec((tb, 2 * LANES), lambda b, g, i: (b * nq + i, g)),
            pl.BlockSpec((seq, LANES), lambda b, g, i: (b, kv_col + g)),
            pl.BlockSpec((tb, LANES), lambda b, g, i: (b * nq + i, g)),
            pl.BlockSpec((None, SUBLANES, LANES), lambda b, g, i: (g, 0, 0)),
        ],
        out_specs=pl.BlockSpec((tb, 2 * LANES), lambda b, g, i: (b * nq + i, g)),
        out_shape=jax.ShapeDtypeStruct((bsz * seq, A_WIDTH), BF16),
        scratch_shapes=[
            pltpu.VMEM((2, WIN + seq, 2 * LANES), BF16),
            pltpu.VMEM(((WIN + seq) // TQ, VT_ROWS, TQ), BF16),
            pltpu.VMEM((A_HG, WIN + TQ, TQ), F32),
        ],
        compiler_params=_cparams(("parallel", "parallel", "arbitrary")),
        name="window_attention",
    )(a0, a0, gl, slopes)


def _outproj0_kernel(oc_ref, os_ref, ow_ref, z_ref, x_ref, gate_ref, w_ref, o_ref):
    o = oc_ref[...].astype(F32) + os_ref[...].astype(F32) + ow_ref[...].astype(F32)
    y = (o * _silu(z_ref[...].astype(F32))).astype(BF16)
    o_ref[...] = x_ref[...] + gate_ref[...] * _dot(y, w_ref[...])


def _outproj0(oc, osl, ow, a0, x2d, gate, w, seq, z_col):
    m, d = x2d.shape
    tm = TM_OUT
    per_b = seq // tm
    row = lambda i: (i, 0)
    return pl.pallas_call(
        _outproj0_kernel,
        grid=(m // tm,),
        in_specs=[
            pl.BlockSpec((tm, A_WIDTH), row),
            pl.BlockSpec((tm, A_WIDTH), row),
            pl.BlockSpec((tm, A_WIDTH), row),
            pl.BlockSpec((tm, A_WIDTH), lambda i: (i, z_col)),
            pl.BlockSpec((tm, d), row),
            pl.BlockSpec((None, 1, d), lambda i: (i // per_b, 0, 0)),
            pl.BlockSpec(w.shape, lambda i: (0, 0)),
        ],
        out_specs=pl.BlockSpec((tm, d), row),
        out_shape=jax.ShapeDtypeStruct((m, d), F32),
        compiler_params=_cparams(("parallel",)),
        name="nsa_out_proj",
    )(oc, osl, ow, a0, x2d, gate[:, None, :], w)


def _conv_kernel(x_ref, halo_ref, w_ref, b_ref, ks_ref, o_ref, *, per_b):
    first = (pl.program_id(0) % per_b) == 0
    halo = jnp.where(first, 0.0, halo_ref[...].astype(F32))
    xe = jnp.concatenate([halo, x_ref[...].astype(F32)], axis=0)
    y = xe[SUBLANES:, :] * w_ref[CONV_W - 1:CONV_W, :] + b_ref[...]
    for s in range(1, CONV_W):
        y = y + pltpu.roll(xe, s, 0)[SUBLANES:, :] * w_ref[CONV_W - 1 - s:CONV_W - s, :]
    o_ref[...] = (_silu(y) * ks_ref[...]).astype(o_ref.dtype)


def _conv_silu(a1, conv_w, conv_b, kscale, seq):
    m = a1.shape[0]
    tm, tc = TM_CONV, TC_CONV
    per_b = seq // tm
    hb = tm // SUBLANES
    kern = functools.partial(_conv_kernel, per_b=per_b)
    return pl.pallas_call(
        kern,
        grid=(m // tm, B_QK // tc),
        in_specs=[
            pl.BlockSpec((tm, tc), lambda i, j: (i, j)),
            pl.BlockSpec((SUBLANES, tc), lambda i, j: (jnp.maximum(i * hb - 1, 0), j)),
            pl.BlockSpec((CONV_W, tc), lambda i, j: (0, j)),
            pl.BlockSpec((1, tc), lambda i, j: (0, j)),
            pl.BlockSpec((1, tc), lambda i, j: (0, j)),
        ],
        out_specs=pl.BlockSpec((tm, tc), lambda i, j: (i, j)),
        out_shape=jax.ShapeDtypeStruct((m, B_QK), BF16),
        compiler_params=_cparams(("parallel", "parallel")),
        name="causal_conv_silu",
    )(a1, a1, conv_w, conv_b.reshape(1, B_QK), kscale)


def _mlstm_kernel(q_ref, k_ref, v_ref, g_ref, gb_ref, hg_ref, o_ref, c_ref, n_ref, m_ref):
    ln = q_ref.shape[0]
    nh = B_HEADS

    @pl.when(pl.program_id(1) == 0)
    def _():
        c_ref[...] = jnp.zeros(c_ref.shape, F32)
        n_ref[...] = jnp.zeros(n_ref.shape, F32)
        m_ref[...] = jnp.zeros(m_ref.shape, F32)

    lane = lax.broadcasted_iota(jnp.int32, (ln, LANES), 1)
    gi = (g_ref[:, 0:LANES] + gb_ref[:, 0:LANES]) * LOG2E
    gf = g_ref[:, LANES:2 * LANES] + gb_ref[:, LANES:2 * LANES]
    lf = (jnp.minimum(gf, 0.0) - jnp.log1p(jnp.exp(-jnp.abs(gf)))) * LOG2E
    r = lax.broadcasted_iota(jnp.int32, (ln, ln), 0)
    c = lax.broadcasted_iota(jnp.int32, (ln, ln), 1)
    causal = c <= r
    bcum = _dot(jnp.where(causal, 1.0, 0.0), lf, precision=HIGHEST)
    rel_t = (gi - bcum).T
    pm = rel_t[0:SUBLANES, :]
    lane_r = lax.broadcasted_iota(jnp.int32, pm.shape, 1)
    step = 1
    while step < ln:
        pm = jnp.maximum(pm, jnp.where(lane_r >= step, pltpu.roll(pm, step, 1), NEG))
        step *= 2
    pmax = jnp.concatenate([pm, jnp.zeros((LANES - SUBLANES, ln), F32)], axis=0).T
    m_st = m_ref[...]
    g_tot = bcum[ln - 1:ln, :]
    m_inter = bcum + m_st
    m_j = jnp.maximum(bcum + pmax, m_inter)
    e_arg = bcum - m_j
    w_inter = jnp.exp2(m_inter - m_j)
    a_all = g_tot - bcum + gi
    m_loc = jnp.max(a_all, axis=0, keepdims=True)
    w_all = jnp.exp2(a_all - m_loc)
    m_new = jnp.maximum(g_tot + m_st, m_loc)
    s_old = jnp.exp2(g_tot + m_st - m_new)
    s_new = jnp.exp2(m_loc - m_new)
    m_ref[...] = m_new

    def head(ref, h, width):
        return ref[:, h * width:(h + 1) * width]

    def onehot_col(h):
        return jnp.where(lane == h, 1.0, 0.0).astype(BF16)

    sqk = [_dot_nt(head(q_ref, h, B_DK), head(k_ref, h, B_DK)) for h in range(nh)]
    qc = [_dot(head(q_ref, h, B_DK), c_ref[h].astype(BF16)) for h in range(nh)]
    qn = _dot(head(q_ref, 0, B_DK), n_ref[0].astype(BF16))
    for h in range(1, nh):
        qn = qn + _dot(head(q_ref, h, B_DK), n_ref[h].astype(BF16))

    smats, kwts = [], []
    rowsum = jnp.zeros((ln, LANES), F32)
    for h in range(nh):
        log_w = jnp.where(causal, e_arg[:, h:h + 1] + rel_t[h:h + 1, :], NEG)
        smat = sqk[h] * jnp.exp2(log_w)
        rowsum = jnp.where(lane == h, jnp.sum(smat, axis=1, keepdims=True), rowsum)
        smats.append(smat.astype(BF16))
        kw = head(k_ref, h, B_DK).astype(F32) * w_all[:, h:h + 1]
        kwts.append(kw.T.astype(BF16))

    intra = [_dot(smats[h], head(v_ref, h, B_DV)) for h in range(nh)]
    c_loc = [_dot(kwts[h], head(v_ref, h, B_DV)) for h in range(nh)]
    n_loc = [_dot(kwts[h], onehot_col(h)) for h in range(nh)]

    us = []
    msq = jnp.zeros((ln, LANES), F32)
    for h in range(nh):
        c_ref[h] = s_old[:, h:h + 1] * c_ref[h] + s_new[:, h:h + 1] * c_loc[h]
        n_ref[h] = s_old[:, h:h + 1] * n_ref[h] + s_new[:, h:h + 1] * n_loc[h]
        u = intra[h] + w_inter[:, h:h + 1] * qc[h]
        msq = jnp.where(lane == h, jnp.mean(u * u, axis=-1, keepdims=True), msq)
        us.append(u)
    den = rowsum + w_inter * qn
    scale = 1.0 / jnp.maximum(jnp.abs(den), jnp.exp2(-m_j))
    f = scale * lax.rsqrt(scale * scale * msq + EPS)
    for h in range(nh):
        o_ref[:, h * B_DV:(h + 1) * B_DV] = (us[h] * f[:, h:h + 1] * head(hg_ref, h, B_DV)).astype(o_ref.dtype)


def _mlstm(qk, a1, g1, gbias, head_g, bsz, seq):
    ln = ML_CHUNK
    nc = seq // ln
    wq = B_QK // 2
    return pl.pallas_call(
        _mlstm_kernel,
        grid=(bsz, nc),
        in_specs=[
            pl.BlockSpec((ln, wq), lambda b, n: (b * nc + n, 0)),
            pl.BlockSpec((ln, wq), lambda b, n: (b * nc + n, 1)),
            pl.BlockSpec((ln, B_WIDTH), lambda b, n: (b * nc + n, B_QK // B_WIDTH)),
            pl.BlockSpec((ln, 2 * LANES), lambda b, n: (b * nc + n, 0)),
            pl.BlockSpec((1, 2 * LANES), lambda b, n: (0, 0)),
            pl.BlockSpec((1, B_WIDTH), lambda b, n: (0, 0)),
        ],
        out_specs=pl.BlockSpec((ln, B_WIDTH), lambda b, n: (b * nc + n, 0)),
        out_shape=jax.ShapeDtypeStruct((bsz * seq, B_WIDTH), BF16),
        scratch_shapes=[
            pltpu.VMEM((B_HEADS, B_DK, B_DV), F32),
            pltpu.VMEM((B_HEADS, B_DK, LANES), F32),
            pltpu.VMEM((1, LANES), F32),
        ],
        compiler_params=_cparams(("parallel", "arbitrary")),
        name="mlstm_chunkwise",
    )(qk, qk, a1, g1, gbias, head_g.reshape(1, B_WIDTH))


def _outproj1_kernel(h_ref, og_ref, z_ref, x_ref, gate_ref, w_ref, fg_ref, o_ref):
    y = jax.nn.sigmoid(og_ref[...].astype(F32)) * h_ref[...].astype(F32) * _silu(z_ref[...].astype(F32))
    x2 = x_ref[...] + gate_ref[...] * _dot(y.astype(BF16), w_ref[...])
    o_ref[...] = x2 * lax.rsqrt(jnp.mean(x2 * x2, axis=-1, keepdims=True) + EPS) * fg_ref[...]


def _outproj1(hn, a1, x2d, gate, w, final_g, seq):
    m, d = x2d.shape
    tm = TM_OUT
    per_b = seq // tm
    row = lambda i: (i, 0)
    og_col = B_QK // B_WIDTH + 1
    return pl.pallas_call(
        _outproj1_kernel,
        grid=(m // tm,),
        in_specs=[
            pl.BlockSpec((tm, B_WIDTH), row),
            pl.BlockSpec((tm, B_WIDTH), lambda i: (i, og_col)),
            pl.BlockSpec((tm, B_WIDTH), lambda i: (i, og_col + 1)),
            pl.BlockSpec((tm, d), row),
            pl.BlockSpec((None, 1, d), lambda i: (i // per_b, 0, 0)),
            pl.BlockSpec(w.shape, lambda i: (0, 0)),
            pl.BlockSpec((1, d), lambda i: (0, 0)),
        ],
        out_specs=pl.BlockSpec((tm, d), row),
        out_shape=jax.ShapeDtypeStruct((m, d), F32),
        compiler_params=_cparams(("parallel",)),
        name="mlstm_out_proj_final_norm",
    )(hn, a1, a1, x2d, gate[:, None, :], w, final_g.reshape(1, d))


def _alibi_slopes():
    return np.asarray(2.0 ** (-8.0 * np.arange(1, A_HEADS + 1) / A_HEADS), np.float32)


def _selection_overlap_t(seq, cols):
    n_cmp = seq // CMP_STRIDE - CMP_LEN // CMP_STRIDE + 1
    n_sel = seq // SLC_LEN
    c0 = np.arange(n_cmp)[None, :] * CMP_STRIDE
    s0 = np.arange(n_sel)[:, None] * SLC_LEN
    ov = np.clip(np.minimum(c0 + CMP_LEN, s0 + SLC_LEN) - np.maximum(c0, s0), 0, None) / CMP_LEN
    out = np.zeros((n_sel, cols), np.float32)
    out[:, :n_cmp] = ov
    return out


def _layer0_weights(w_in):
    d = w_in.shape[0]
    kv0 = A_WIDTH
    gl0 = kv0 + 6 * A_GROUPS * A_DH
    z0 = gl0 + 3 * A_HEADS
    wq = w_in[:, :A_WIDTH] * (A_DH ** -0.5 * LOG2E)
    wz = w_in[:, z0:z0 + A_WIDTH]
    wkv = w_in[:, kv0:gl0].reshape(d, 6, A_GROUPS, A_DH)
    branches = [wkv[:, 2 * r:2 * r + 2].transpose(0, 2, 1, 3).reshape(d, A_GROUPS * 2 * A_DH) for r in range(3)]
    w0 = jnp.concatenate([wq, wz] + branches, axis=1).astype(BF16)
    wgl = w_in[:, gl0:z0].reshape(d, 3, A_GROUPS, A_HG).transpose(0, 2, 1, 3).reshape(d, A_GROUPS, 3 * A_HG)
    wgl = jnp.pad(wgl, ((0, 0), (0, 0), (0, LANES - 3 * A_HG))).reshape(d, A_GROUPS * LANES).astype(BF16)
    return w0, wgl


def _compress_weights(pe, w1, b1, w2, b2):
    half = CMP_LEN // 2
    w1r = w1.reshape(2, 2, half, A_DH, CMP_HIDDEN)
    wc = jnp.einsum("khldn,kq->lkdqhn", w1r, jnp.eye(2, dtype=w1.dtype))
    wc = wc.reshape(half * 2 * A_DH, 4 * CMP_HIDDEN).astype(BF16)
    pec = pe.reshape(2, 2, half, A_DH).transpose(1, 2, 0, 3).reshape(2, half * 2 * A_DH)
    pec = jnp.pad(pec, ((0, SUBLANES - 2), (0, 0))).astype(BF16)
    b1c = b1.reshape(1, 2 * CMP_HIDDEN)
    w2k = jnp.zeros((CMP_HIDDEN, 2 * LANES), F32)
    w2k = w2k.at[:, 0:A_DH].set(w2[0]).at[:, LANES + A_DH:2 * LANES].set(w2[0])
    b2k = jnp.zeros((1, 2 * LANES), F32)
    b2k = b2k.at[0, 0:A_DH].set(b2[0]).at[0, LANES + A_DH:2 * LANES].set(b2[0])
    w2vt = w2[1].T.astype(BF16)
    b2v = jnp.broadcast_to(b2[1][:, None], (A_DH, LANES))
    return wc, pec, b1c, w2k.astype(BF16), b2k, w2vt, b2v


def _layer1_weights(w_in):
    g0 = B_QK + B_WIDTH
    w_main = jnp.concatenate([w_in[:, :g0], w_in[:, g0 + 2 * B_HEADS:]], axis=1).astype(BF16)
    pad = ((0, 0), (0, LANES - B_HEADS))
    wg = jnp.concatenate([jnp.pad(w_in[:, g0:g0 + B_HEADS], pad),
                          jnp.pad(w_in[:, g0 + B_HEADS:g0 + 2 * B_HEADS], pad)], axis=1).astype(BF16)
    return w_main, wg


def kernel(x, c, ada_w, ada_b, norm_g, final_g, a_w_in, a_cmp_pe, a_cmp_w1, a_cmp_b1, a_cmp_w2, a_cmp_b2,
           a_w_out, b_w_in, b_conv_w, b_conv_b, b_gate_b, b_head_g, b_w_out):
    bsz, seq, d = x.shape
    assert d == D_MODEL and seq % (2 * TM_PROJ) == 0 and ada_w.shape[0] == 2
    m = bsz * seq
    x2d = x.reshape(m, d)
    mod = _ada_mod(c, ada_w, ada_b)

    w0, wgl = _layer0_weights(a_w_in[0])
    a0, gl = _normmod_matmul(x2d, norm_g[0], mod[0, :, d:2 * d], mod[0, :, :d], w0, wgl, seq, TN_PROJ0, "nsa_in_proj")
    z_col = 1
    kv_col = 2 * A_WIDTH // LANES
    n_chunk = seq // CMP_STRIDE
    n_cmp = n_chunk - CMP_LEN // CMP_STRIDE + 1
    n_sel = seq // SLC_LEN
    n_top = min(SLC_TOP, n_sel)
    assert n_sel % SUBLANES == 0 and n_sel <= LANES and n_top >= 3
    kvsrc = a0[:, 2 * A_WIDTH:2 * A_WIDTH + A_GROUPS * LANES]
    xc = kvsrc.reshape(bsz, n_chunk, CMP_STRIDE, A_GROUPS, LANES).transpose(0, 3, 1, 2, 4)
    xc = xc.reshape(bsz, A_GROUPS, n_chunk, CMP_STRIDE * LANES)
    kc, vct = _compress(xc, *_compress_weights(a_cmp_pe[0], a_cmp_w1[0], a_cmp_b1[0], a_cmp_w2[0], a_cmp_b2[0]))
    slopes = jnp.asarray(np.broadcast_to(
        np.pad((_alibi_slopes() * np.float32(LOG2E)).reshape(A_GROUPS, A_HG), ((0, 0), (0, SUBLANES - A_HG)))[:, :, None],
        (A_GROUPS, SUBLANES, LANES)))
    ovt = jnp.asarray(_selection_overlap_t(seq, n_chunk))
    o_cmp, sb = _cmp_attn(a0, kc, vct, gl, slopes, ovt, bsz, seq, n_cmp, n_sel, n_top)
    o_slc = _slc_attn(a0, sb, gl, slopes, bsz, seq, kv_col + A_GROUPS)
    o_win = _win_attn(a0, gl, slopes, bsz, seq, kv_col + 2 * A_GROUPS)
    x1 = _outproj0(o_cmp, o_slc, o_win, a0, x2d, mod[0, :, 2 * d:], a_w_out[0].astype(BF16), seq, z_col)

    w1m, w1g = _layer1_weights(b_w_in[0])
    a1, g1 = _normmod_matmul(x1, norm_g[1], mod[1, :, d:2 * d], mod[1, :, :d], w1m, w1g, seq, TN_PROJ1, "mlstm_in_proj")
    kscale = jnp.concatenate([jnp.ones((1, B_QK // 2), F32), jnp.full((1, B_QK // 2), B_DK ** -0.5, F32)], axis=1)
    qk = _conv_silu(a1, b_conv_w[0], b_conv_b[0], kscale, seq)
    gbias = jnp.pad(b_gate_b[0], ((0, 0), (0, LANES - B_HEADS))).reshape(1, 2 * LANES)
    y1 = _mlstm(qk, a1, g1, gbias, b_head_g[0], bsz, seq)
    out = _outproj1(y1, a1, x1, mod[1, :, 2 * d:], b_w_out[0].astype(BF16), final_g, seq)
    return out.reshape(bsz, seq, d)
```

```python
import functools

import numpy as np
import jax
import jax.numpy as jnp
from jax import lax
from jax.experimental import pallas as pl
from jax.experimental.pallas import tpu as pltpu

F32 = jnp.float32
BF16 = jnp.bfloat16
HIGHEST = lax.Precision.HIGHEST

EPS = 1e-6
NEG = -1e30
LOG2E = float(np.log2(np.e))

D_MODEL = 1024
A_HEADS = 16
A_GROUPS = 4
A_HG = A_HEADS // A_GROUPS
A_DH = 64
A_WIDTH = A_HEADS * A_DH
CMP_LEN = 32
CMP_STRIDE = 16
CMP_HIDDEN = 256
SLC_LEN = 64
SLC_SHIFT = 6
SLC_TOP = 16
WIN = 512
B_HEADS = 8
B_DK = 128
B_DV = 256
B_WIDTH = B_HEADS * B_DV
B_QK = 2 * B_HEADS * B_DK
CONV_W = 4

LANES = 128
SUBLANES = 8
VMEM_LIMIT = 56 * 1024 * 1024

TQ = 256
TQ_SLC = 512
TQ_CMP = 1024
WIN_SUBTILES = 2
VT_ROWS = A_DH + 16
ML_CHUNK = 512
TM_PROJ = 1024
TN_PROJ0 = 1792
TN_PROJ1 = 2048
NORM_ROWS = 256
TM_OUT = 512
OUT_CHUNK = 256
TM_CONV = 1024
TC_CONV = 512

assert WIN % TQ == 0 and 1 << SLC_SHIFT == SLC_LEN


def _dot(a, b, **kw):
    return jnp.dot(a, b, preferred_element_type=F32, **kw)


def _dot_nt(a, b):
    return lax.dot_general(a, b, (((1,), (1,)), ((), ())), preferred_element_type=F32)


def _cparams(sem):
    return pltpu.CompilerParams(dimension_semantics=sem, vmem_limit_bytes=VMEM_LIMIT)


def _silu(x):
    return x * jax.nn.sigmoid(x)


def _ada_kernel(c_ref, w_ref, b_ref, o_ref):
    o_ref[...] = _dot(c_ref[...], w_ref[...], precision=HIGHEST) + b_ref[...]


def _ada_mod(c, ada_w, ada_b):
    depth, d, d3 = ada_w.shape
    bsz = c.shape[0]
    nj = d3 // d
    return pl.pallas_call(
        _ada_kernel,
        grid=(depth, nj),
        in_specs=[
            pl.BlockSpec((bsz, d), lambda i, j: (0, 0)),
            pl.BlockSpec((None, d, d), lambda i, j: (i, 0, j)),
            pl.BlockSpec((None, 1, d), lambda i, j: (i, 0, j)),
        ],
        out_specs=pl.BlockSpec((None, bsz, d), lambda i, j: (i, 0, j)),
        out_shape=jax.ShapeDtypeStruct((depth, bsz, d3), F32),
        compiler_params=_cparams(("parallel", "parallel")),
        name="ada_mod",
    )(c, ada_w, ada_b.reshape(depth, 1, d3))


def _normmod_kernel(x_ref, g_ref, sc_ref, sh_ref, w_ref, wg_ref, o_ref, og_ref, h_ref):
    first = pl.program_id(1) == 0

    @pl.when(first)
    def _():
        for r0 in range(0, x_ref.shape[0], NORM_ROWS):
            rows = slice(r0, r0 + NORM_ROWS)
            x = x_ref[rows, :]
            xn = x * lax.rsqrt(jnp.mean(x * x, axis=-1, keepdims=True) + EPS)
            hb = ((xn * g_ref[...]) * (1.0 + sc_ref[...]) + sh_ref[...]).astype(BF16)
            h_ref[rows, :] = hb
            og_ref[rows, :] = _dot(hb, wg_ref[...])
            o_ref[rows, :] = _dot(hb, w_ref[...]).astype(o_ref.dtype)

    @pl.when(jnp.logical_not(first))
    def _():
        o_ref[...] = _dot(h_ref[...], w_ref[...]).astype(o_ref.dtype)


def _normmod_matmul(x2d, g, scale, shift, w, wg, seq, tn, name):
    m, d = x2d.shape
    n = w.shape[1]
    ng = wg.shape[1]
    tm = TM_PROJ
    assert m % tm == 0 and n % tn == 0 and seq % tm == 0
    per_b = seq // tm
    w_tiles = w.reshape(d, n // tn, tn).transpose(1, 0, 2)
    return pl.pallas_call(
        _normmod_kernel,
        grid=(m // tm, n // tn),
        in_specs=[
            pl.BlockSpec((tm, d), lambda i, j: (i, 0)),
            pl.BlockSpec((1, d), lambda i, j: (0, 0)),
            pl.BlockSpec((None, 1, d), lambda i, j: (i // per_b, 0, 0)),
            pl.BlockSpec((None, 1, d), lambda i, j: (i // per_b, 0, 0)),
            pl.BlockSpec((None, d, tn), lambda i, j: (j, 0, 0)),
            pl.BlockSpec((d, ng), lambda i, j: (0, 0)),
        ],
        out_specs=[
            pl.BlockSpec((tm, tn), lambda i, j: (i, j)),
            pl.BlockSpec((tm, ng), lambda i, j: (i, 0)),
        ],
        out_shape=[jax.ShapeDtypeStruct((m, n), BF16), jax.ShapeDtypeStruct((m, ng), F32)],
        scratch_shapes=[pltpu.VMEM((tm, d), BF16)],
        compiler_params=_cparams(("parallel", "arbitrary")),
        name=name,
    )(x2d, g.reshape(1, d), scale[:, None, :], shift[:, None, :], w_tiles, wg)


def _compress_kernel(x_ref, wc_ref, pe_ref, b1_ref, w2k_ref, b2k_ref, w2vt_ref, b2v_ref, ok_ref, ov_ref):
    pb = _dot(pe_ref[...], wc_ref[...])
    n = x_ref.shape[0]
    hid = CMP_HIDDEN
    tok = A_GROUPS * LANES
    for g in range(A_GROUPS):
        xg = jnp.concatenate([x_ref[:, l * tok + g * LANES:l * tok + (g + 1) * LANES] for l in range(CMP_STRIDE)], axis=1)
        a = _dot(xg, wc_ref[...])
        hs = []
        for kv in range(2):
            c0 = kv * 2 * hid
            first = a[:, c0:c0 + hid]
            second = pltpu.roll(a[:, c0 + hid:c0 + 2 * hid], n - 1, 0)
            bias = pb[0:1, c0:c0 + hid] + pb[1:2, c0 + hid:c0 + 2 * hid] + b1_ref[:, kv * hid:(kv + 1) * hid]
            hs.append(jax.nn.gelu(first + second + bias))
        ok_ref[g] = (_dot(hs[0].astype(BF16), w2k_ref[...]) + b2k_ref[...]).astype(ok_ref.dtype)
        ov_ref[g] = (_dot_nt(w2vt_ref[...], hs[1].astype(BF16)) + b2v_ref[:, 0:1]).astype(ov_ref.dtype)


def _compress(xb, wc, pec, b1c, w2k, b2k, w2vt, b2v):
    bsz, nchunk, width = xb.shape
    full = lambda a: pl.BlockSpec(a.shape, lambda b: (0,) * a.ndim)
    return pl.pallas_call(
        _compress_kernel,
        grid=(bsz,),
        in_specs=[pl.BlockSpec((None, nchunk, width), lambda b: (b, 0, 0)),
                  full(wc), full(pec), full(b1c), full(w2k), full(b2k), full(w2vt), full(b2v)],
        out_specs=[
            pl.BlockSpec((None, A_GROUPS, nchunk, 2 * LANES), lambda b: (b, 0, 0, 0)),
            pl.BlockSpec((None, A_GROUPS, A_DH, nchunk), lambda b: (b, 0, 0, 0)),
        ],
        out_shape=[
            jax.ShapeDtypeStruct((bsz, A_GROUPS, nchunk, 2 * LANES), BF16),
            jax.ShapeDtypeStruct((bsz, A_GROUPS, A_DH, nchunk), BF16),
        ],
        compiler_params=_cparams(("parallel",)),
        name="compress_tokens",
    )(xb, wc, pec, b1c, w2k, b2k, w2vt, b2v)


def _cmp_attn_kernel(q_ref, k_ref, vt_ref, gl_ref, sl_ref, ovt_ref, o_ref, sb_ref, *, n_cmp, n_sel, n_top):
    tq = q_ref.shape[0]
    ncol = k_ref.shape[0]
    q0 = pl.program_id(2) * tq
    tpos = lax.broadcasted_iota(jnp.int32, (ncol, tq), 1) + q0
    cidx = lax.broadcasted_iota(jnp.int32, (ncol, tq), 0)
    valid = (cidx * CMP_STRIDE + (CMP_LEN - 1) <= tpos) & (cidx < n_cmp)
    gate_t = jax.nn.sigmoid(gl_ref[...]).T
    vt = vt_ref[...]
    lane_k = lax.broadcasted_iota(jnp.int32, (ncol, LANES), 1)
    kpos = (lax.broadcasted_iota(jnp.int32, (ncol, LANES), 0) * CMP_STRIDE + (CMP_LEN - 1)).astype(F32)
    kpos_hi = jnp.floor(kpos * (1.0 / 256.0)) * 256.0
    lane_q = lax.broadcasted_iota(jnp.int32, (tq, LANES), 1)
    kaug, qaug = [], []
    for e in range(2):
        aug = jnp.where((lane_k >= 6 * e) & (lane_k < 6 * e + 3), kpos_hi, 0.0)
        aug = jnp.where((lane_k >= 6 * e + 3) & (lane_k < 6 * e + 6), kpos - kpos_hi, aug)
        kaug.append(aug.astype(BF16))
    for p in range(2):
        aug = jnp.zeros((tq, LANES), F32)
        for e in range(2):
            slope = jnp.broadcast_to(sl_ref[2 * p + e:2 * p + e + 1, 0:1], (tq, LANES))
            for j, piece in enumerate(_bf16_pieces(slope)):
                aug = jnp.where((lane_q == 6 * e + j) | (lane_q == 6 * e + 3 + j), piece, aug)
        qaug.append(aug.astype(BF16))
    scores = []
    for hg in range(A_HG):
        p, e = divmod(hg, 2)
        keys = jnp.concatenate([k_ref[:, e * LANES:(e + 1) * LANES], kaug[e]], axis=1)
        qrs = jnp.concatenate([q_ref[:, p * LANES:(p + 1) * LANES], qaug[p]], axis=1)
        scores.append(_dot_nt(keys, qrs))
    probs = []
    psum = jnp.zeros((ncol, tq), F32)
    for hg in range(A_HG):
        s = jnp.where(valid, scores[hg], NEG)
        m = jnp.max(s, axis=0, keepdims=True)
        m = jnp.where(m > 0.5 * NEG, m, 0.0)
        pr = jnp.exp2(s - m)
        pr = pr * (1.0 / jnp.maximum(jnp.sum(pr, axis=0, keepdims=True), 1e-30))
        psum = psum + pr
        probs.append(pr.astype(BF16))
    outs = [_dot(vt, probs[hg]) * gate_t[hg:hg + 1, :] for hg in range(A_HG)]
    o_ref[...] = jnp.concatenate(outs, axis=0).T.astype(o_ref.dtype)

    imp = _dot(ovt_ref[...], psum, precision=HIGHEST)
    blk = lax.broadcasted_iota(jnp.int32, (n_sel, tq), 0)
    blkf = blk.astype(F32)
    cur = lax.shift_right_logical(lax.broadcasted_iota(jnp.int32, (n_sel, tq), 1) + q0, SLC_SHIFT)
    forced = (blk == 0) | (blk == cur) | (blk == cur - 1)
    score = jnp.where(forced | (blk > cur), -1.0, imp)
    sel = jnp.where(forced, 1.0, 0.0)
    for _ in range(n_top - 3):
        mx = jnp.max(score, axis=0, keepdims=True)
        idx = jnp.min(jnp.where(score == mx, blkf, 1e9), axis=0, keepdims=True)
        pick = blkf == idx
        sel = jnp.where(pick, 1.0, sel)
        score = jnp.where(pick, -3.0, score)
    bias_t = jnp.where(sel > 0.0, 0.0, NEG)
    bias_t = jnp.concatenate([bias_t, jnp.zeros((LANES - n_sel, tq), F32)], axis=0)
    sb_ref[...] = bias_t.T.astype(sb_ref.dtype)


def _cmp_attn(a0, kc, vct, gl, slopes, ovt, bsz, seq, n_cmp, n_sel, n_top):
    tq = TQ_CMP
    nq = seq // tq
    ncol = kc.shape[2]
    kern = functools.partial(_cmp_attn_kernel, n_cmp=n_cmp, n_sel=n_sel, n_top=n_top)
    return pl.pallas_call(
        kern,
        grid=(bsz, A_GROUPS, nq),
        in_specs=[
            pl.BlockSpec((tq, 2 * LANES), lambda b, g, i: (b * nq + i, g)),
            pl.BlockSpec((None, None, ncol, 2 * LANES), lambda b, g, i: (b, g, 0, 0)),
            pl.BlockSpec((None, None, A_DH, ncol), lambda b, g, i: (b, g, 0, 0)),
            pl.BlockSpec((tq, LANES), lambda b, g, i: (b * nq + i, g)),
            pl.BlockSpec((None, SUBLANES, LANES), lambda b, g, i: (g, 0, 0)),
            pl.BlockSpec(ovt.shape, lambda b, g, i: (0, 0)),
        ],
        out_specs=[
            pl.BlockSpec((tq, 2 * LANES), lambda b, g, i: (b * nq + i, g)),
            pl.BlockSpec((None, None, tq, LANES), lambda b, g, i: (b, g, i, 0)),
        ],
        out_shape=[
            jax.ShapeDtypeStruct((bsz * seq, A_WIDTH), BF16),
            jax.ShapeDtypeStruct((bsz, A_GROUPS, seq, LANES), BF16),
        ],
        compiler_params=_cparams(("parallel", "parallel", "parallel")),
        name="cmp_attention_topk",
    )(a0, kc, vct, gl, slopes, ovt)


def _stage_kv(kv_ref, ka_ref, vt_ref, row0, chunk, aug_fn):
    seq = kv_ref.shape[0]
    for t in range(seq // chunk):
        kvf = kv_ref[t * chunk:(t + 1) * chunk, :].astype(F32)
        low = lax.broadcasted_iota(jnp.int32, kvf.shape, 1) < A_DH
        rolled = pltpu.roll(kvf, A_DH, 1)
        r0 = row0 + t * chunk
        ka_ref[0, r0:r0 + chunk, 0:LANES] = jnp.where(low, kvf, 0.0).astype(BF16)
        ka_ref[1, r0:r0 + chunk, 0:LANES] = jnp.where(low, 0.0, rolled).astype(BF16)
        ka_ref[0, r0:r0 + chunk, LANES:2 * LANES] = aug_fn(t, 0)
        ka_ref[1, r0:r0 + chunk, LANES:2 * LANES] = aug_fn(t, 1)
        vt_ref[row0 // chunk + t, 0:A_DH, :] = kvf.T[A_DH:, :].astype(BF16)
        ones_rows = lax.broadcasted_iota(jnp.int32, (VT_ROWS - A_DH, chunk), 0) == 0
        vt_ref[row0 // chunk + t, A_DH:VT_ROWS, :] = jnp.where(ones_rows, 1.0, 0.0).astype(BF16)


def _bf16_pieces(x):
    p1 = x.astype(BF16)
    r1 = x - p1.astype(F32)
    p2 = r1.astype(BF16)
    p3 = (r1 - p2.astype(F32)).astype(BF16)
    return [p1.astype(F32), p2.astype(F32), p3.astype(F32)]


def _slc_kernel(q_ref, sb_ref, kv_ref, gl_ref, sl_ref, o_ref,
                ka_ref, vt_ref, ex_ref, bd_ref, m_ref, acc_ref, qa_ref, sa_ref, sb2_ref):
    tq = q_ref.shape[0]
    tk = tq
    i = pl.program_id(2)

    lane = lax.broadcasted_iota(jnp.int32, (tq, LANES), 1)
    pos = lax.broadcasted_iota(jnp.int32, (tq, LANES), 0).astype(F32)
    pos_hi = jnp.floor(pos * (1.0 / 256.0)) * 256.0
    key_pieces = [pos] if tk <= 256 else [pos_hi, pos - pos_hi]
    nkp = len(key_pieces)
    stride = 3 * nkp + 3
    assert tk <= 256 * 256 and A_DH + 2 * stride <= LANES

    @pl.when(i == 0)
    def _stage():
        def key_aug(t, e):
            blk = lax.shift_right_logical(lax.broadcasted_iota(jnp.int32, (tk, LANES), 0) + t * tk, SLC_SHIFT)
            aug = jnp.where(blk == lane, 1.0, 0.0)
            base = A_DH + stride * e
            for kp, piece in enumerate(key_pieces):
                aug = jnp.where((lane >= base + 3 * kp) & (lane < base + 3 * kp + 3), piece, aug)
            aug = jnp.where((lane >= base + 3 * nkp) & (lane < base + stride), -1.0, aug)
            return aug.astype(BF16)

        _stage_kv(kv_ref, ka_ref, vt_ref, 0, tk, key_aug)
        for p in range(2):
            ex = jnp.zeros((tq, LANES), F32)
            for e in range(2):
                base = A_DH + stride * e
                slope = jnp.broadcast_to(sl_ref[2 * p + e:2 * p + e + 1, 0:1], (tq, LANES))
                for j, piece in enumerate(_bf16_pieces(slope)):
                    for kp in range(nkp):
                        ex = jnp.where(lane == base + 3 * kp + j, piece, ex)
                for j, piece in enumerate(_bf16_pieces(slope * pos)):
                    ex = jnp.where(lane == base + 3 * nkp + j, piece, ex)
            ex_ref[p] = ex.astype(BF16)
        kk = lax.broadcasted_iota(jnp.int32, (tk, tq), 0)
        qq = lax.broadcasted_iota(jnp.int32, (tk, tq), 1)
        bd_ref[...] = jnp.where(kk <= qq, 0.0, NEG)

    sb = sb_ref[...]
    for p in range(2):
        qa_ref[p, :, 0:LANES] = q_ref[:, p * LANES:(p + 1) * LANES]
        qa_ref[p, :, LANES:2 * LANES] = jnp.where(lane < A_DH, sb, ex_ref[p])
    m_ref[...] = jnp.full(m_ref.shape, NEG, F32)
    acc_ref[...] = jnp.zeros(acc_ref.shape, F32)

    def scores(kt, s_ref):
        k0 = pl.multiple_of(kt * tk, tk)
        for hg in range(A_HG):
            s_ref[hg] = _dot_nt(ka_ref[hg % 2, pl.ds(k0, tk), :], qa_ref[hg // 2])

    def softmax_pv(kt, s_ref, causal=False):
        off = ((i - kt) * tk).astype(F32)
        vt = vt_ref[kt]
        m_all = m_ref[...]
        probs, alphas, m_rows = [], [], []
        for hg in range(A_HG):
            cst = -sl_ref[hg:hg + 1, 0:1] * off
            s = s_ref[hg] + bd_ref[...] if causal else s_ref[hg]
            m_prev = m_all[hg:hg + 1, :]
            m_new = jnp.maximum(m_prev, jnp.max(s, axis=0, keepdims=True) + cst)
            alphas.append(jnp.exp2(m_prev - m_new))
            probs.append(jnp.exp2(s - (m_new - cst)).astype(BF16))
            m_rows.append(m_new)
        m_ref[...] = jnp.concatenate(m_rows + [jnp.zeros((SUBLANES - A_HG, tq), F32)], axis=0)
        for hg in range(A_HG):
            acc_ref[hg] = acc_ref[hg] * alphas[hg] + _dot(vt, probs[hg])

    scores(0, sa_ref)

    def body(j, carry):
        kt = 2 * j
        scores(kt + 1, sb2_ref)
        softmax_pv(kt, sa_ref)
        scores(kt + 2, sa_ref)
        softmax_pv(kt + 1, sb2_ref)
        return carry

    lax.fori_loop(0, i // 2, body, 0)

    @pl.when(i % 2 == 0)
    def _():
        softmax_pv(i, sa_ref, causal=True)

    @pl.when(i % 2 == 1)
    def _():
        scores(i, sb2_ref)
        softmax_pv(i - 1, sa_ref)
        softmax_pv(i, sb2_ref, causal=True)

    gate_t = jax.nn.sigmoid(gl_ref[...]).T
    outs = []
    for hg in range(A_HG):
        w = gate_t[A_HG + hg:A_HG + hg + 1, :] / jnp.maximum(acc_ref[hg, A_DH:A_DH + 1, :], 1e-30)
        outs.append(acc_ref[hg, 0:A_DH, :] * w)
    o_ref[...] = jnp.concatenate(outs, axis=0).T.astype(o_ref.dtype)


def _slc_attn(a0, sb, gl, slopes, bsz, seq, kv_col):
    tq = TQ_SLC
    nq = seq // tq
    return pl.pallas_call(
        _slc_kernel,
        grid=(bsz, A_GROUPS, nq),
        in_specs=[
            pl.BlockSpec((tq, 2 * LANES), lambda b, g, i: (b * nq + i, g)),
            pl.BlockSpec((None, None, tq, LANES), lambda b, g, i: (b, g, i, 0)),
            pl.BlockSpec((seq, LANES), lambda b, g, i: (b, kv_col + g)),
            pl.BlockSpec((tq, LANES), lambda b, g, i: (b * nq + i, g)),
            pl.BlockSpec((None, SUBLANES, LANES), lambda b, g, i: (g, 0, 0)),
        ],
        out_specs=pl.BlockSpec((tq, 2 * LANES), lambda b, g, i: (b * nq + i, g)),
        out_shape=jax.ShapeDtypeStruct((bsz * seq, A_WIDTH), BF16),
        scratch_shapes=[
            pltpu.VMEM((2, seq, 2 * LANES), BF16),
            pltpu.VMEM((seq // tq, VT_ROWS, tq), BF16),
            pltpu.VMEM((2, tq, LANES), BF16),
            pltpu.VMEM((tq, tq), F32),
            pltpu.VMEM((SUBLANES, tq), F32),
            pltpu.VMEM((A_HG, VT_ROWS, tq), F32),
            pltpu.VMEM((2, tq, 2 * LANES), BF16),
            pltpu.VMEM((A_HG, tq, tq), F32),
            pltpu.VMEM((A_HG, tq, tq), F32),
        ],
        compiler_params=_cparams(("parallel", "parallel", "arbitrary")),
        name="selected_attention",
    )(a0, sb, a0, gl, slopes)


def _win_kernel(q_ref, kv_ref, gl_ref, sl_ref, o_ref, ka_ref, vt_ref, bw_ref):
    tq = TQ
    nsub = q_ref.shape[0] // tq
    wk = WIN + tq
    npad = WIN // tq
    i = pl.program_id(2)

    @pl.when(i == 0)
    def _stage():
        lane_p = lax.broadcasted_iota(jnp.int32, (WIN, LANES), 1)
        for e in range(2):
            ka_ref[e, 0:WIN, 0:LANES] = jnp.zeros((WIN, LANES), BF16)
            ka_ref[e, 0:WIN, LANES:2 * LANES] = jnp.where(lane_p == 0, 1.0, 0.0).astype(BF16)
        for t in range(npad):
            vt_ref[t] = jnp.zeros(vt_ref.shape[1:], BF16)
        _stage_kv(kv_ref, ka_ref, vt_ref, WIN, tq, lambda t, e: jnp.zeros((tq, LANES), BF16))
        kk = lax.broadcasted_iota(jnp.int32, (wk, tq), 0)
        qq = lax.broadcasted_iota(jnp.int32, (wk, tq), 1)
        dist = qq + WIN - kk
        ok = (dist >= 0) & (dist < WIN)
        distf = dist.astype(F32)
        for hg in range(A_HG):
            bw_ref[hg] = jnp.where(ok, -sl_ref[hg:hg + 1, 0:1] * distf, NEG)

    lane = lax.broadcasted_iota(jnp.int32, (tq, LANES), 1)
    negrow = jnp.where(lane == 0, NEG, 0.0).astype(BF16)
    gate_t = jax.nn.sigmoid(gl_ref[...]).T
    units = [(sub, hg) for sub in range(nsub) for hg in range(A_HG)]
    scores = []
    for sub in range(nsub):
        k0 = pl.multiple_of((i * nsub + sub) * tq, tq)
        rows = slice(sub * tq, (sub + 1) * tq)
        qa = [jnp.concatenate([q_ref[rows, p * LANES:(p + 1) * LANES], negrow], axis=1) for p in range(2)]
        scores += [_dot_nt(ka_ref[hg % 2, pl.ds(k0, wk), :], qa[hg // 2]) for hg in range(A_HG)]
    probs = []
    for u, (sub, hg) in enumerate(units):
        s = scores[u] + bw_ref[hg]
        probs.append(jnp.exp2(s - jnp.max(s, axis=0, keepdims=True)).astype(BF16))
    outs = []
    for u, (sub, hg) in enumerate(units):
        t0 = i * nsub + sub
        o = _dot(vt_ref[t0], probs[u][0:tq, :])
        for j in range(1, wk // tq):
            o = o + _dot(vt_ref[t0 + j], probs[u][j * tq:(j + 1) * tq, :])
        gate = gate_t[2 * A_HG + hg:2 * A_HG + hg + 1, sub * tq:(sub + 1) * tq]
        outs.append(o[0:A_DH, :] * (gate / jnp.maximum(o[A_DH:A_DH + 1, :], 1e-30)))
    for sub in range(nsub):
        tile = jnp.concatenate(outs[sub * A_HG:(sub + 1) * A_HG], axis=0).T
        o_ref[sub * tq:(sub + 1) * tq, :] = tile.astype(o_ref.dtype)


def _win_attn(a0, gl, slopes, bsz, seq, kv_col):
    tb = WIN_SUBTILES * TQ
    nq = seq // tb
    return pl.pallas_call(
        _win_kernel,
        grid=(bsz, A_GROUPS, nq),
        in_specs=[
            pl.BlockSpec((tb, 2 * LANES), lambda b, g, i: (b * nq + i, g)),
            pl.BlockSpec((seq, LANES), lambda b, g, i: (b, kv_col + g)),
            pl.BlockSpec((tb, LANES), lambda b, g, i: (b * nq + i, g)),
            pl.BlockSpec((None, SUBLANES, LANES), lambda b, g, i: (g, 0, 0)),
        ],
        out_specs=pl.BlockSpec((tb, 2 * LANES), lambda b, g, i: (b * nq + i, g)),
        out_shape=jax.ShapeDtypeStruct((bsz * seq, A_WIDTH), BF16),
        scratch_shapes=[
            pltpu.VMEM((2, WIN + seq, 2 * LANES), BF16),
            pltpu.VMEM(((WIN + seq) // TQ, VT_ROWS, TQ), BF16),
            pltpu.VMEM((A_HG, WIN + TQ, TQ), F32),
        ],
        compiler_params=_cparams(("parallel", "parallel", "arbitrary")),
        name="window_attention",
    )(a0, a0, gl, slopes)


def _outproj0_kernel(oc_ref, os_ref, ow_ref, z_ref, x_ref, gate_ref, w_ref, o_ref):
    acc = None
    for c0 in range(0, oc_ref.shape[1], OUT_CHUNK):
        cols = slice(c0, c0 + OUT_CHUNK)
        o = oc_ref[:, cols].astype(F32) + os_ref[:, cols].astype(F32) + ow_ref[:, cols].astype(F32)
        y = (o * _silu(z_ref[:, cols].astype(F32))).astype(BF16)
        part = _dot(y, w_ref[cols, :])
        acc = part if acc is None else acc + part
    o_ref[...] = x_ref[...] + gate_ref[...] * acc


def _outproj0(oc, osl, ow, a0, x2d, gate, w, seq, z_col):
    m, d = x2d.shape
    tm = TM_OUT
    per_b = seq // tm
    row = lambda i: (i, 0)
    return pl.pallas_call(
        _outproj0_kernel,
        grid=(m // tm,),
        in_specs=[
            pl.BlockSpec((tm, A_WIDTH), row),
            pl.BlockSpec((tm, A_WIDTH), row),
            pl.BlockSpec((tm, A_WIDTH), row),
            pl.BlockSpec((tm, A_WIDTH), lambda i: (i, z_col)),
            pl.BlockSpec((tm, d), row),
            pl.BlockSpec((None, 1, d), lambda i: (i // per_b, 0, 0)),
            pl.BlockSpec(w.shape, lambda i: (0, 0)),
        ],
        out_specs=pl.BlockSpec((tm, d), row),
        out_shape=jax.ShapeDtypeStruct((m, d), F32),
        compiler_params=_cparams(("parallel",)),
        name="nsa_out_proj",
    )(oc, osl, ow, a0, x2d, gate[:, None, :], w)


def _conv_kernel(x_ref, halo_ref, w_ref, b_ref, ks_ref, o_ref, *, per_b):
    first = (pl.program_id(0) % per_b) == 0
    halo = jnp.where(first, 0.0, halo_ref[...].astype(F32))
    xe = jnp.concatenate([halo, x_ref[...].astype(F32)], axis=0)
    y = xe[SUBLANES:, :] * w_ref[CONV_W - 1:CONV_W, :] + b_ref[...]
    for s in range(1, CONV_W):
        y = y + pltpu.roll(xe, s, 0)[SUBLANES:, :] * w_ref[CONV_W - 1 - s:CONV_W - s, :]
    o_ref[...] = (_silu(y) * ks_ref[...]).astype(o_ref.dtype)


def _conv_silu(a1, conv_w, conv_b, kscale, seq):
    m = a1.shape[0]
    tm, tc = TM_CONV, TC_CONV
    per_b = seq // tm
    hb = tm // SUBLANES
    kern = functools.partial(_conv_kernel, per_b=per_b)
    return pl.pallas_call(
        kern,
        grid=(m // tm, B_QK // tc),
        in_specs=[
            pl.BlockSpec((tm, tc), lambda i, j: (i, j)),
            pl.BlockSpec((SUBLANES, tc), lambda i, j: (jnp.maximum(i * hb - 1, 0), j)),
            pl.BlockSpec((CONV_W, tc), lambda i, j: (0, j)),
            pl.BlockSpec((1, tc), lambda i, j: (0, j)),
            pl.BlockSpec((1, tc), lambda i, j: (0, j)),
        ],
        out_specs=pl.BlockSpec((tm, tc), lambda i, j: (i, j)),
        out_shape=jax.ShapeDtypeStruct((m, B_QK), BF16),
        compiler_params=_cparams(("parallel", "parallel")),
        name="causal_conv_silu",
    )(a1, a1, conv_w, conv_b.reshape(1, B_QK), kscale)


def _mlstm_kernel(q_ref, k_ref, v_ref, g_ref, gb_ref, hg_ref, o_ref, c_ref, n_ref, m_ref):
    ln = q_ref.shape[0]
    nh = B_HEADS

    @pl.when(pl.program_id(1) == 0)
    def _():
        c_ref[...] = jnp.zeros(c_ref.shape, F32)
        n_ref[...] = jnp.zeros(n_ref.shape, F32)
        m_ref[...] = jnp.zeros(m_ref.shape, F32)

    lane = lax.broadcasted_iota(jnp.int32, (ln, LANES), 1)
    gi = (g_ref[:, 0:LANES] + gb_ref[:, 0:LANES]) * LOG2E
    gf = g_ref[:, LANES:2 * LANES] + gb_ref[:, LANES:2 * LANES]
    lf = (jnp.minimum(gf, 0.0) - jnp.log1p(jnp.exp(-jnp.abs(gf)))) * LOG2E
    r = lax.broadcasted_iota(jnp.int32, (ln, ln), 0)
    c = lax.broadcasted_iota(jnp.int32, (ln, ln), 1)
    causal = c <= r
    bcum = _dot(jnp.where(causal, 1.0, 0.0), lf, precision=HIGHEST)
    rel_t = (gi - bcum).T
    pm = rel_t[0:SUBLANES, :]
    lane_r = lax.broadcasted_iota(jnp.int32, pm.shape, 1)
    step = 1
    while step < ln:
        pm = jnp.maximum(pm, jnp.where(lane_r >= step, pltpu.roll(pm, step, 1), NEG))
        step *= 2
    pmax = jnp.concatenate([pm, jnp.zeros((LANES - SUBLANES, ln), F32)], axis=0).T
    m_st = m_ref[...]
    g_tot = bcum[ln - 1:ln, :]
    m_inter = bcum + m_st
    m_j = jnp.maximum(bcum + pmax, m_inter)
    e_arg = bcum - m_j
    w_inter = jnp.exp2(m_inter - m_j)
    a_all = g_tot - bcum + gi
    m_loc = jnp.max(a_all, axis=0, keepdims=True)
    w_all = jnp.exp2(a_all - m_loc)
    m_new = jnp.maximum(g_tot + m_st, m_loc)
    s_old = jnp.exp2(g_tot + m_st - m_new)
    s_new = jnp.exp2(m_loc - m_new)
    m_ref[...] = m_new

    def head(ref, h, width):
        return ref[:, h * width:(h + 1) * width]

    def onehot_col(h):
        return jnp.where(lane == h, 1.0, 0.0).astype(BF16)

    sqk = [_dot_nt(head(q_ref, h, B_DK), head(k_ref, h, B_DK)) for h in range(nh)]
    qc = [_dot(head(q_ref, h, B_DK), c_ref[h].astype(BF16)) for h in range(nh)]
    qn = _dot(head(q_ref, 0, B_DK), n_ref[0].astype(BF16))
    for h in range(1, nh):
        qn = qn + _dot(head(q_ref, h, B_DK), n_ref[h].astype(BF16))

    smats, kwts = [], []
    rowsum = jnp.zeros((ln, LANES), F32)
    for h in range(nh):
        log_w = jnp.where(causal, e_arg[:, h:h + 1] + rel_t[h:h + 1, :], NEG)
        smat = sqk[h] * jnp.exp2(log_w)
        rowsum = jnp.where(lane == h, jnp.sum(smat, axis=1, keepdims=True), rowsum)
        smats.append(smat.astype(BF16))
        kw = head(k_ref, h, B_DK).astype(F32) * w_all[:, h:h + 1]
        kwts.append(kw.T.astype(BF16))

    intra = [_dot(smats[h], head(v_ref, h, B_DV)) for h in range(nh)]
    c_loc = [_dot(kwts[h], head(v_ref, h, B_DV)) for h in range(nh)]
    n_loc = [_dot(kwts[h], onehot_col(h)) for h in range(nh)]

    us = []
    msq = jnp.zeros((ln, LANES), F32)
    for h in range(nh):
        c_ref[h] = s_old[:, h:h + 1] * c_ref[h] + s_new[:, h:h + 1] * c_loc[h]
        n_ref[h] = s_old[:, h:h + 1] * n_ref[h] + s_new[:, h:h + 1] * n_loc[h]
        u = intra[h] + w_inter[:, h:h + 1] * qc[h]
        msq = jnp.where(lane == h, jnp.mean(u * u, axis=-1, keepdims=True), msq)
        us.append(u)
    den = rowsum + w_inter * qn
    scale = 1.0 / jnp.maximum(jnp.abs(den), jnp.exp2(-m_j))
    f = scale * lax.rsqrt(scale * scale * msq + EPS)
    for h in range(nh):
        o_ref[:, h * B_DV:(h + 1) * B_DV] = (us[h] * f[:, h:h + 1] * head(hg_ref, h, B_DV)).astype(o_ref.dtype)


def _mlstm(qk, a1, g1, gbias, head_g, bsz, seq):
    ln = ML_CHUNK
    nc = seq // ln
    wq = B_QK // 2
    return pl.pallas_call(
        _mlstm_kernel,
        grid=(bsz, nc),
        in_specs=[
            pl.BlockSpec((ln, wq), lambda b, n: (b * nc + n, 0)),
            pl.BlockSpec((ln, wq), lambda b, n: (b * nc + n, 1)),
            pl.BlockSpec((ln, B_WIDTH), lambda b, n: (b * nc + n, B_QK // B_WIDTH)),
            pl.BlockSpec((ln, 2 * LANES), lambda b, n: (b * nc + n, 0)),
            pl.BlockSpec((1, 2 * LANES), lambda b, n: (0, 0)),
            pl.BlockSpec((1, B_WIDTH), lambda b, n: (0, 0)),
        ],
        out_specs=pl.BlockSpec((ln, B_WIDTH), lambda b, n: (b * nc + n, 0)),
        out_shape=jax.ShapeDtypeStruct((bsz * seq, B_WIDTH), BF16),
        scratch_shapes=[
            pltpu.VMEM((B_HEADS, B_DK, B_DV), F32),
            pltpu.VMEM((B_HEADS, B_DK, LANES), F32),
            pltpu.VMEM((1, LANES), F32),
        ],
        compiler_params=_cparams(("parallel", "arbitrary")),
        name="mlstm_chunkwise",
    )(qk, qk, a1, g1, gbias, head_g.reshape(1, B_WIDTH))


def _outproj1_kernel(h_ref, og_ref, z_ref, x_ref, gate_ref, w_ref, fg_ref, o_ref):
    acc = None
    for c0 in range(0, h_ref.shape[1], OUT_CHUNK):
        cols = slice(c0, c0 + OUT_CHUNK)
        y = (jax.nn.sigmoid(og_ref[:, cols].astype(F32)) * h_ref[:, cols].astype(F32)
             * _silu(z_ref[:, cols].astype(F32)))
        part = _dot(y.astype(BF16), w_ref[cols, :])
        acc = part if acc is None else acc + part
    x2 = x_ref[...] + gate_ref[...] * acc
    o_ref[...] = x2 * lax.rsqrt(jnp.mean(x2 * x2, axis=-1, keepdims=True) + EPS) * fg_ref[...]


def _outproj1(hn, a1, x2d, gate, w, final_g, seq):
    m, d = x2d.shape
    tm = TM_OUT
    per_b = seq // tm
    row = lambda i: (i, 0)
    og_col = B_QK // B_WIDTH + 1
    return pl.pallas_call(
        _outproj1_kernel,
        grid=(m // tm,),
        in_specs=[
            pl.BlockSpec((tm, B_WIDTH), row),
            pl.BlockSpec((tm, B_WIDTH), lambda i: (i, og_col)),
            pl.BlockSpec((tm, B_WIDTH), lambda i: (i, og_col + 1)),
            pl.BlockSpec((tm, d), row),
            pl.BlockSpec((None, 1, d), lambda i: (i // per_b, 0, 0)),
            pl.BlockSpec(w.shape, lambda i: (0, 0)),
            pl.BlockSpec((1, d), lambda i: (0, 0)),
        ],
        out_specs=pl.BlockSpec((tm, d), row),
        out_shape=jax.ShapeDtypeStruct((m, d), F32),
        compiler_params=_cparams(("parallel",)),
        name="mlstm_out_proj_final_norm",
    )(hn, a1, a1, x2d, gate[:, None, :], w, final_g.reshape(1, d))


def _alibi_slopes():
    return np.asarray(2.0 ** (-8.0 * np.arange(1, A_HEADS + 1) / A_HEADS), np.float32)


def _selection_overlap_t(seq, cols):
    n_cmp = seq // CMP_STRIDE - CMP_LEN // CMP_STRIDE + 1
    n_sel = seq // SLC_LEN
    c0 = np.arange(n_cmp)[None, :] * CMP_STRIDE
    s0 = np.arange(n_sel)[:, None] * SLC_LEN
    ov = np.clip(np.minimum(c0 + CMP_LEN, s0 + SLC_LEN) - np.maximum(c0, s0), 0, None) / CMP_LEN
    out = np.zeros((n_sel, cols), np.float32)
    out[:, :n_cmp] = ov
    return out


def _layer0_weights(w_in):
    d = w_in.shape[0]
    kv0 = A_WIDTH
    gl0 = kv0 + 6 * A_GROUPS * A_DH
    z0 = gl0 + 3 * A_HEADS
    wq = w_in[:, :A_WIDTH] * (A_DH ** -0.5 * LOG2E)
    wz = w_in[:, z0:z0 + A_WIDTH]
    wkv = w_in[:, kv0:gl0].reshape(d, 6, A_GROUPS, A_DH)
    branches = [wkv[:, 2 * r:2 * r + 2].transpose(0, 2, 1, 3).reshape(d, A_GROUPS * 2 * A_DH) for r in range(3)]
    w0 = jnp.concatenate([wq, wz] + branches, axis=1).astype(BF16)
    wgl = w_in[:, gl0:z0].reshape(d, 3, A_GROUPS, A_HG).transpose(0, 2, 1, 3).reshape(d, A_GROUPS, 3 * A_HG)
    wgl = jnp.pad(wgl, ((0, 0), (0, 0), (0, LANES - 3 * A_HG))).reshape(d, A_GROUPS * LANES).astype(BF16)
    return w0, wgl


def _compress_weights(pe, w1, b1, w2, b2):
    half = CMP_LEN // 2
    w1r = w1.reshape(2, 2, half, A_DH, CMP_HIDDEN)
    wc = jnp.einsum("khldn,kq->lkdqhn", w1r, jnp.eye(2, dtype=w1.dtype))
    wc = wc.reshape(half * 2 * A_DH, 4 * CMP_HIDDEN).astype(BF16)
    pec = pe.reshape(2, 2, half, A_DH).transpose(1, 2, 0, 3).reshape(2, half * 2 * A_DH)
    pec = jnp.pad(pec, ((0, SUBLANES - 2), (0, 0))).astype(BF16)
    b1c = b1.reshape(1, 2 * CMP_HIDDEN)
    w2k = jnp.zeros((CMP_HIDDEN, 2 * LANES), F32)
    w2k = w2k.at[:, 0:A_DH].set(w2[0]).at[:, LANES + A_DH:2 * LANES].set(w2[0])
    b2k = jnp.zeros((1, 2 * LANES), F32)
    b2k = b2k.at[0, 0:A_DH].set(b2[0]).at[0, LANES + A_DH:2 * LANES].set(b2[0])
    w2vt = w2[1].T.astype(BF16)
    b2v = jnp.broadcast_to(b2[1][:, None], (A_DH, LANES))
    return wc, pec, b1c, w2k.astype(BF16), b2k, w2vt, b2v


def _layer1_weights(w_in):
    g0 = B_QK + B_WIDTH
    w_main = jnp.concatenate([w_in[:, :g0], w_in[:, g0 + 2 * B_HEADS:]], axis=1).astype(BF16)
    pad = ((0, 0), (0, LANES - B_HEADS))
    wg = jnp.concatenate([jnp.pad(w_in[:, g0:g0 + B_HEADS], pad),
                          jnp.pad(w_in[:, g0 + B_HEADS:g0 + 2 * B_HEADS], pad)], axis=1).astype(BF16)
    return w_main, wg


def kernel(x, c, ada_w, ada_b, norm_g, final_g, a_w_in, a_cmp_pe, a_cmp_w1, a_cmp_b1, a_cmp_w2, a_cmp_b2,
           a_w_out, b_w_in, b_conv_w, b_conv_b, b_gate_b, b_head_g, b_w_out):
    bsz, seq, d = x.shape
    assert d == D_MODEL and seq % (2 * TM_PROJ) == 0 and ada_w.shape[0] == 2
    m = bsz * seq
    x2d = x.reshape(m, d)
    mod = _ada_mod(c, ada_w, ada_b)

    w0, wgl = _layer0_weights(a_w_in[0])
    a0, gl = _normmod_matmul(x2d, norm_g[0], mod[0, :, d:2 * d], mod[0, :, :d], w0, wgl, seq, TN_PROJ0, "nsa_in_proj")
    z_col = 1
    kv_col = 2 * A_WIDTH // LANES
    n_chunk = seq // CMP_STRIDE
    n_cmp = n_chunk - CMP_LEN // CMP_STRIDE + 1
    n_sel = seq // SLC_LEN
    n_top = min(SLC_TOP, n_sel)
    assert n_sel % SUBLANES == 0 and n_sel <= LANES and n_top >= 3
    kvsrc = a0[:, 2 * A_WIDTH:2 * A_WIDTH + A_GROUPS * LANES]
    xb = kvsrc.reshape(bsz, n_chunk, CMP_STRIDE * A_GROUPS * LANES)
    kc, vct = _compress(xb, *_compress_weights(a_cmp_pe[0], a_cmp_w1[0], a_cmp_b1[0], a_cmp_w2[0], a_cmp_b2[0]))
    slopes = jnp.asarray(np.broadcast_to(
        np.pad((_alibi_slopes() * np.float32(LOG2E)).reshape(A_GROUPS, A_HG), ((0, 0), (0, SUBLANES - A_HG)))[:, :, None],
        (A_GROUPS, SUBLANES, LANES)))
    ovt = jnp.asarray(_selection_overlap_t(seq, n_chunk))
    o_cmp, sb = _cmp_attn(a0, kc, vct, gl, slopes, ovt, bsz, seq, n_cmp, n_sel, n_top)
    o_slc = _slc_attn(a0, sb, gl, slopes, bsz, seq, kv_col + A_GROUPS)
    o_win = _win_attn(a0, gl, slopes, bsz, seq, kv_col + 2 * A_GROUPS)
    x1 = _outproj0(o_cmp, o_slc, o_win, a0, x2d, mod[0, :, 2 * d:], a_w_out[0].astype(BF16), seq, z_col)

    w1m, w1g = _layer1_weights(b_w_in[0])
    a1, g1 = _normmod_matmul(x1, norm_g[1], mod[1, :, d:2 * d], mod[1, :, :d], w1m, w1g, seq, TN_PROJ1, "mlstm_in_proj")
    kscale = jnp.concatenate([jnp.ones((1, B_QK // 2), F32), jnp.full((1, B_QK // 2), B_DK ** -0.5, F32)], axis=1)
    qk = _conv_silu(a1, b_conv_w[0], b_conv_b[0], kscale, seq)
    gbias = jnp.pad(b_gate_b[0], ((0, 0), (0, LANES - B_HEADS))).reshape(1, 2 * LANES)
    y1 = _mlstm(qk, a1, g1, gbias, b_head_g[0], bsz, seq)
    out = _outproj1(y1, a1, x1, mod[1, :, 2 * d:], b_w_out[0].astype(BF16), final_g, seq)
    return out.reshape(bsz, seq, d)
```

```python
import functools

import numpy as np
import jax
import jax.numpy as jnp
from jax import lax
from jax.experimental import pallas as pl
from jax.experimental.pallas import tpu as pltpu

F32 = jnp.float32
BF16 = jnp.bfloat16
HIGHEST = lax.Precision.HIGHEST

EPS = 1e-6
NEG = -1e30
LOG2E = float(np.log2(np.e))

D_MODEL = 1024
A_HEADS = 16
A_GROUPS = 4
A_HG = A_HEADS // A_GROUPS
A_DH = 64
A_WIDTH = A_HEADS * A_DH
CMP_LEN = 32
CMP_STRIDE = 16
CMP_HIDDEN = 256
SLC_LEN = 64
SLC_SHIFT = 6
SLC_TOP = 16
WIN = 512
B_HEADS = 8
B_DK = 128
B_DV = 256
B_WIDTH = B_HEADS * B_DV
B_QK = 2 * B_HEADS * B_DK
CONV_W = 4

LANES = 128
SUBLANES = 8
VMEM_LIMIT = 56 * 1024 * 1024

TQ = 256
TQ_SLC = 512
TQ_CMP = 1024
WIN_SUBTILES = 2
VT_ROWS = A_DH + 16
ML_CHUNK = 512
TM_PROJ = 1024
TN_PROJ0 = 1792
TN_PROJ1 = 2048
NORM_ROWS = 256
TM_OUT = 512
OUT_CHUNK = 256
TM_CONV = 1024
TC_CONV = 512

assert WIN % TQ == 0 and 1 << SLC_SHIFT == SLC_LEN


def _dot(a, b, **kw):
    return jnp.dot(a, b, preferred_element_type=F32, **kw)


def _dot_nt(a, b):
    return lax.dot_general(a, b, (((1,), (1,)), ((), ())), preferred_element_type=F32)


def _cparams(sem):
    return pltpu.CompilerParams(dimension_semantics=sem, vmem_limit_bytes=VMEM_LIMIT)


def _silu(x):
    return x * jax.nn.sigmoid(x)


def _ada_kernel(c_ref, w_ref, b_ref, o_ref):
    o_ref[...] = _dot(c_ref[...], w_ref[...], precision=HIGHEST) + b_ref[...]


def _ada_mod(c, ada_w, ada_b):
    depth, d, d3 = ada_w.shape
    bsz = c.shape[0]
    nj = d3 // d
    return pl.pallas_call(
        _ada_kernel,
        grid=(depth, nj),
        in_specs=[
            pl.BlockSpec((bsz, d), lambda i, j: (0, 0)),
            pl.BlockSpec((None, d, d), lambda i, j: (i, 0, j)),
            pl.BlockSpec((None, 1, d), lambda i, j: (i, 0, j)),
        ],
        out_specs=pl.BlockSpec((None, bsz, d), lambda i, j: (i, 0, j)),
        out_shape=jax.ShapeDtypeStruct((depth, bsz, d3), F32),
        compiler_params=_cparams(("parallel", "parallel")),
        name="ada_mod",
    )(c, ada_w, ada_b.reshape(depth, 1, d3))


def _normmod_kernel(x_ref, g_ref, sc_ref, sh_ref, w_ref, wg_ref, o_ref, og_ref, h_ref):
    first = pl.program_id(1) == 0

    @pl.when(first)
    def _():
        for r0 in range(0, x_ref.shape[0], NORM_ROWS):
            rows = slice(r0, r0 + NORM_ROWS)
            x = x_ref[rows, :]
            xn = x * lax.rsqrt(jnp.mean(x * x, axis=-1, keepdims=True) + EPS)
            hb = ((xn * g_ref[...]) * (1.0 + sc_ref[...]) + sh_ref[...]).astype(BF16)
            h_ref[rows, :] = hb
            og_ref[rows, :] = _dot(hb, wg_ref[...])
            o_ref[rows, :] = _dot(hb, w_ref[...]).astype(o_ref.dtype)

    @pl.when(jnp.logical_not(first))
    def _():
        o_ref[...] = _dot(h_ref[...], w_ref[...]).astype(o_ref.dtype)


def _normmod_matmul(x2d, g, scale, shift, w, wg, seq, tn, name):
    m, d = x2d.shape
    n = w.shape[1]
    ng = wg.shape[1]
    tm = TM_PROJ
    assert m % tm == 0 and n % tn == 0 and seq % tm == 0
    per_b = seq // tm
    w_tiles = w.reshape(d, n // tn, tn).transpose(1, 0, 2)
    return pl.pallas_call(
        _normmod_kernel,
        grid=(m // tm, n // tn),
        in_specs=[
            pl.BlockSpec((tm, d), lambda i, j: (i, 0)),
            pl.BlockSpec((1, d), lambda i, j: (0, 0)),
            pl.BlockSpec((None, 1, d), lambda i, j: (i // per_b, 0, 0)),
            pl.BlockSpec((None, 1, d), lambda i, j: (i // per_b, 0, 0)),
            pl.BlockSpec((None, d, tn), lambda i, j: (j, 0, 0)),
            pl.BlockSpec((d, ng), lambda i, j: (0, 0)),
        ],
        out_specs=[
            pl.BlockSpec((tm, tn), lambda i, j: (i, j)),
            pl.BlockSpec((tm, ng), lambda i, j: (i, 0)),
        ],
        out_shape=[jax.ShapeDtypeStruct((m, n), BF16), jax.ShapeDtypeStruct((m, ng), F32)],
        scratch_shapes=[pltpu.VMEM((tm, d), BF16)],
        compiler_params=_cparams(("parallel", "arbitrary")),
        name=name,
    )(x2d, g.reshape(1, d), scale[:, None, :], shift[:, None, :], w_tiles, wg)


def _compress_kernel(x_ref, wc_ref, pe_ref, b1_ref, w2k_ref, b2k_ref, w2vt_ref, b2v_ref, ok_ref, ov_ref, xf_ref):
    pb = _dot(pe_ref[...], wc_ref[...])
    n = x_ref.shape[0] // CMP_STRIDE
    hid = CMP_HIDDEN
    for g in range(A_GROUPS):
        xf_ref[g] = x_ref[:, g * LANES:(g + 1) * LANES].astype(F32)
        xg = jnp.concatenate([xf_ref[g, pl.ds(l, n, stride=CMP_STRIDE), :]
                              for l in range(CMP_STRIDE)], axis=1).astype(BF16)
        a = _dot(xg, wc_ref[...])
        hs = []
        for kv in range(2):
            c0 = kv * 2 * hid
            first = a[:, c0:c0 + hid]
            second = pltpu.roll(a[:, c0 + hid:c0 + 2 * hid], n - 1, 0)
            bias = pb[0:1, c0:c0 + hid] + pb[1:2, c0 + hid:c0 + 2 * hid] + b1_ref[:, kv * hid:(kv + 1) * hid]
            hs.append(jax.nn.gelu(first + second + bias))
        ok_ref[g] = (_dot(hs[0].astype(BF16), w2k_ref[...]) + b2k_ref[...]).astype(ok_ref.dtype)
        ov_ref[g] = (_dot_nt(w2vt_ref[...], hs[1].astype(BF16)) + b2v_ref[:, 0:1]).astype(ov_ref.dtype)


def _compress(a0, col, bsz, seq, wc, pec, b1c, w2k, b2k, w2vt, b2v):
    nchunk = seq // CMP_STRIDE
    width = A_GROUPS * LANES
    full = lambda a: pl.BlockSpec(a.shape, lambda b: (0,) * a.ndim)
    return pl.pallas_call(
        _compress_kernel,
        grid=(bsz,),
        in_specs=[pl.BlockSpec((seq, width), lambda b: (b, col)),
                  full(wc), full(pec), full(b1c), full(w2k), full(b2k), full(w2vt), full(b2v)],
        out_specs=[
            pl.BlockSpec((None, A_GROUPS, nchunk, 2 * LANES), lambda b: (b, 0, 0, 0)),
            pl.BlockSpec((None, A_GROUPS, A_DH, nchunk), lambda b: (b, 0, 0, 0)),
        ],
        out_shape=[
            jax.ShapeDtypeStruct((bsz, A_GROUPS, nchunk, 2 * LANES), BF16),
            jax.ShapeDtypeStruct((bsz, A_GROUPS, A_DH, nchunk), BF16),
        ],
        scratch_shapes=[pltpu.VMEM((A_GROUPS, seq, LANES), F32)],
        compiler_params=_cparams(("parallel",)),
        name="compress_tokens",
    )(a0, wc, pec, b1c, w2k, b2k, w2vt, b2v)


def _cmp_attn_kernel(q_ref, k_ref, vt_ref, gl_ref, sl_ref, ovt_ref, o_ref, sb_ref, *, n_cmp, n_sel, n_top):
    tq = q_ref.shape[0]
    ncol = k_ref.shape[0]
    q0 = pl.program_id(2) * tq
    tpos = lax.broadcasted_iota(jnp.int32, (ncol, tq), 1) + q0
    cidx = lax.broadcasted_iota(jnp.int32, (ncol, tq), 0)
    valid = (cidx * CMP_STRIDE + (CMP_LEN - 1) <= tpos) & (cidx < n_cmp)
    gate_t = jax.nn.sigmoid(gl_ref[...]).T
    vt = vt_ref[...]
    lane_k = lax.broadcasted_iota(jnp.int32, (ncol, LANES), 1)
    kpos = (lax.broadcasted_iota(jnp.int32, (ncol, LANES), 0) * CMP_STRIDE + (CMP_LEN - 1)).astype(F32)
    kpos_hi = jnp.floor(kpos * (1.0 / 256.0)) * 256.0
    lane_q = lax.broadcasted_iota(jnp.int32, (tq, LANES), 1)
    kaug, qaug = [], []
    for e in range(2):
        aug = jnp.where((lane_k >= 6 * e) & (lane_k < 6 * e + 3), kpos_hi, 0.0)
        aug = jnp.where((lane_k >= 6 * e + 3) & (lane_k < 6 * e + 6), kpos - kpos_hi, aug)
        kaug.append(aug.astype(BF16))
    for p in range(2):
        aug = jnp.zeros((tq, LANES), F32)
        for e in range(2):
            slope = jnp.broadcast_to(sl_ref[2 * p + e:2 * p + e + 1, 0:1], (tq, LANES))
            for j, piece in enumerate(_bf16_pieces(slope)):
                aug = jnp.where((lane_q == 6 * e + j) | (lane_q == 6 * e + 3 + j), piece, aug)
        qaug.append(aug.astype(BF16))
    scores = []
    for hg in range(A_HG):
        p, e = divmod(hg, 2)
        keys = jnp.concatenate([k_ref[:, e * LANES:(e + 1) * LANES], kaug[e]], axis=1)
        qrs = jnp.concatenate([q_ref[:, p * LANES:(p + 1) * LANES], qaug[p]], axis=1)
        scores.append(_dot_nt(keys, qrs))
    probs = []
    psum = jnp.zeros((ncol, tq), F32)
    for hg in range(A_HG):
        s = jnp.where(valid, scores[hg], NEG)
        m = jnp.max(s, axis=0, keepdims=True)
        m = jnp.where(m > 0.5 * NEG, m, 0.0)
        pr = jnp.exp2(s - m)
        pr = pr * (1.0 / jnp.maximum(jnp.sum(pr, axis=0, keepdims=True), 1e-30))
        psum = psum + pr
        probs.append(pr.astype(BF16))
    outs = [_dot(vt, probs[hg]) * gate_t[hg:hg + 1, :] for hg in range(A_HG)]
    o_ref[...] = jnp.concatenate(outs, axis=0).T.astype(o_ref.dtype)

    imp = _dot(ovt_ref[...], psum, precision=HIGHEST)
    blk = lax.broadcasted_iota(jnp.int32, (n_sel, tq), 0)
    blkf = blk.astype(F32)
    cur = lax.shift_right_logical(lax.broadcasted_iota(jnp.int32, (n_sel, tq), 1) + q0, SLC_SHIFT)
    forced = (blk == 0) | (blk == cur) | (blk == cur - 1)
    score = jnp.where(forced | (blk > cur), -1.0, imp)
    sel = jnp.where(forced, 1.0, 0.0)
    for _ in range(n_top - 3):
        mx = jnp.max(score, axis=0, keepdims=True)
        idx = jnp.min(jnp.where(score == mx, blkf, 1e9), axis=0, keepdims=True)
        pick = blkf == idx
        sel = jnp.where(pick, 1.0, sel)
        score = jnp.where(pick, -3.0, score)
    bias_t = jnp.where(sel > 0.0, 0.0, NEG)
    bias_t = jnp.concatenate([bias_t, jnp.zeros((LANES - n_sel, tq), F32)], axis=0)
    sb_ref[...] = bias_t.T.astype(sb_ref.dtype)


def _cmp_attn(a0, kc, vct, gl, slopes, ovt, bsz, seq, n_cmp, n_sel, n_top):
    tq = TQ_CMP
    nq = seq // tq
    ncol = kc.shape[2]
    kern = functools.partial(_cmp_attn_kernel, n_cmp=n_cmp, n_sel=n_sel, n_top=n_top)
    return pl.pallas_call(
        kern,
        grid=(bsz, A_GROUPS, nq),
        in_specs=[
            pl.BlockSpec((tq, 2 * LANES), lambda b, g, i: (b * nq + i, g)),
            pl.BlockSpec((None, None, ncol, 2 * LANES), lambda b, g, i: (b, g, 0, 0)),
            pl.BlockSpec((None, None, A_DH, ncol), lambda b, g, i: (b, g, 0, 0)),
            pl.BlockSpec((tq, LANES), lambda b, g, i: (b * nq + i, g)),
            pl.BlockSpec((None, SUBLANES, LANES), lambda b, g, i: (g, 0, 0)),
            pl.BlockSpec(ovt.shape, lambda b, g, i: (0, 0)),
        ],
        out_specs=[
            pl.BlockSpec((tq, 2 * LANES), lambda b, g, i: (b * nq + i, g)),
            pl.BlockSpec((None, None, tq, LANES), lambda b, g, i: (b, g, i, 0)),
        ],
        out_shape=[
            jax.ShapeDtypeStruct((bsz * seq, A_WIDTH), BF16),
            jax.ShapeDtypeStruct((bsz, A_GROUPS, seq, LANES), BF16),
        ],
        compiler_params=_cparams(("parallel", "parallel", "parallel")),
        name="cmp_attention_topk",
    )(a0, kc, vct, gl, slopes, ovt)


def _stage_kv(kv_ref, ka_ref, vt_ref, row0, chunk, aug_fn):
    seq = kv_ref.shape[0]
    for t in range(seq // chunk):
        kvf = kv_ref[t * chunk:(t + 1) * chunk, :].astype(F32)
        low = lax.broadcasted_iota(jnp.int32, kvf.shape, 1) < A_DH
        rolled = pltpu.roll(kvf, A_DH, 1)
        r0 = row0 + t * chunk
        ka_ref[0, r0:r0 + chunk, 0:LANES] = jnp.where(low, kvf, 0.0).astype(BF16)
        ka_ref[1, r0:r0 + chunk, 0:LANES] = jnp.where(low, 0.0, rolled).astype(BF16)
        ka_ref[0, r0:r0 + chunk, LANES:2 * LANES] = aug_fn(t, 0)
        ka_ref[1, r0:r0 + chunk, LANES:2 * LANES] = aug_fn(t, 1)
        vt_ref[row0 // chunk + t, 0:A_DH, :] = kvf.T[A_DH:, :].astype(BF16)
        ones_rows = lax.broadcasted_iota(jnp.int32, (VT_ROWS - A_DH, chunk), 0) == 0
        vt_ref[row0 // chunk + t, A_DH:VT_ROWS, :] = jnp.where(ones_rows, 1.0, 0.0).astype(BF16)


def _bf16_pieces(x):
    p1 = x.astype(BF16)
    r1 = x - p1.astype(F32)
    p2 = r1.astype(BF16)
    p3 = (r1 - p2.astype(F32)).astype(BF16)
    return [p1.astype(F32), p2.astype(F32), p3.astype(F32)]


def _slc_kernel(q_ref, sb_ref, kv_ref, gl_ref, sl_ref, o_ref,
                ka_ref, vt_ref, ex_ref, bd_ref, m_ref, acc_ref, qa_ref, sa_ref, sb2_ref):
    tq = q_ref.shape[0]
    tk = tq
    i = pl.program_id(2)

    lane = lax.broadcasted_iota(jnp.int32, (tq, LANES), 1)
    pos = lax.broadcasted_iota(jnp.int32, (tq, LANES), 0).astype(F32)
    pos_hi = jnp.floor(pos * (1.0 / 256.0)) * 256.0
    key_pieces = [pos] if tk <= 256 else [pos_hi, pos - pos_hi]
    nkp = len(key_pieces)
    stride = 3 * nkp + 3
    assert tk <= 256 * 256 and A_DH + 2 * stride <= LANES

    @pl.when(i == 0)
    def _stage():
        def key_aug(t, e):
            blk = lax.shift_right_logical(lax.broadcasted_iota(jnp.int32, (tk, LANES), 0) + t * tk, SLC_SHIFT)
            aug = jnp.where(blk == lane, 1.0, 0.0)
            base = A_DH + stride * e
            for kp, piece in enumerate(key_pieces):
                aug = jnp.where((lane >= base + 3 * kp) & (lane < base + 3 * kp + 3), piece, aug)
            aug = jnp.where((lane >= base + 3 * nkp) & (lane < base + stride), -1.0, aug)
            return aug.astype(BF16)

        _stage_kv(kv_ref, ka_ref, vt_ref, 0, tk, key_aug)
        for p in range(2):
            ex = jnp.zeros((tq, LANES), F32)
            for e in range(2):
                base = A_DH + stride * e
                slope = jnp.broadcast_to(sl_ref[2 * p + e:2 * p + e + 1, 0:1], (tq, LANES))
                for j, piece in enumerate(_bf16_pieces(slope)):
                    for kp in range(nkp):
                        ex = jnp.where(lane == base + 3 * kp + j, piece, ex)
                for j, piece in enumerate(_bf16_pieces(slope * pos)):
                    ex = jnp.where(lane == base + 3 * nkp + j, piece, ex)
            ex_ref[p] = ex.astype(BF16)
        kk = lax.broadcasted_iota(jnp.int32, (tk, tq), 0)
        qq = lax.broadcasted_iota(jnp.int32, (tk, tq), 1)
        bd_ref[...] = jnp.where(kk <= qq, 0.0, NEG)

    sb = sb_ref[...]
    for p in range(2):
        qa_ref[p, :, 0:LANES] = q_ref[:, p * LANES:(p + 1) * LANES]
        qa_ref[p, :, LANES:2 * LANES] = jnp.where(lane < A_DH, sb, ex_ref[p])
    m_ref[...] = jnp.full(m_ref.shape, NEG, F32)
    acc_ref[...] = jnp.zeros(acc_ref.shape, F32)

    def scores(kt, s_ref):
        k0 = pl.multiple_of(kt * tk, tk)
        for hg in range(A_HG):
            s_ref[hg] = _dot_nt(ka_ref[hg % 2, pl.ds(k0, tk), :], qa_ref[hg // 2])

    def softmax_pv(kt, s_ref, causal=False):
        off = ((i - kt) * tk).astype(F32)
        vt = vt_ref[kt]
        m_all = m_ref[...]
        probs, alphas, m_rows = [], [], []
        for hg in range(A_HG):
            cst = -sl_ref[hg:hg + 1, 0:1] * off
            s = s_ref[hg] + bd_ref[...] if causal else s_ref[hg]
            m_prev = m_all[hg:hg + 1, :]
            m_new = jnp.maximum(m_prev, jnp.max(s, axis=0, keepdims=True) + cst)
            alphas.append(jnp.exp2(m_prev - m_new))
            probs.append(jnp.exp2(s - (m_new - cst)).astype(BF16))
            m_rows.append(m_new)
        m_ref[...] = jnp.concatenate(m_rows + [jnp.zeros((SUBLANES - A_HG, tq), F32)], axis=0)
        for hg in range(A_HG):
            acc_ref[hg] = acc_ref[hg] * alphas[hg] + _dot(vt, probs[hg])

    scores(0, sa_ref)

    def body(j, carry):
        kt = 2 * j
        scores(kt + 1, sb2_ref)
        softmax_pv(kt, sa_ref)
        scores(kt + 2, sa_ref)
        softmax_pv(kt + 1, sb2_ref)
        return carry

    lax.fori_loop(0, i // 2, body, 0)

    @pl.when(i % 2 == 0)
    def _():
        softmax_pv(i, sa_ref, causal=True)

    @pl.when(i % 2 == 1)
    def _():
        scores(i, sb2_ref)
        softmax_pv(i - 1, sa_ref)
        softmax_pv(i, sb2_ref, causal=True)

    gate_t = jax.nn.sigmoid(gl_ref[...]).T
    outs = []
    for hg in range(A_HG):
        w = gate_t[A_HG + hg:A_HG + hg + 1, :] / jnp.maximum(acc_ref[hg, A_DH:A_DH + 1, :], 1e-30)
        outs.append(acc_ref[hg, 0:A_DH, :] * w)
    o_ref[...] = jnp.concatenate(outs, axis=0).T.astype(o_ref.dtype)


def _slc_attn(a0, sb, gl, slopes, bsz, seq, kv_col):
    tq = TQ_SLC
    nq = seq // tq
    return pl.pallas_call(
        _slc_kernel,
        grid=(bsz, A_GROUPS, nq),
        in_specs=[
            pl.BlockSpec((tq, 2 * LANES), lambda b, g, i: (b * nq + i, g)),
            pl.BlockSpec((None, None, tq, LANES), lambda b, g, i: (b, g, i, 0)),
            pl.BlockSpec((seq, LANES), lambda b, g, i: (b, kv_col + g)),
            pl.BlockSpec((tq, LANES), lambda b, g, i: (b * nq + i, g)),
            pl.BlockSpec((None, SUBLANES, LANES), lambda b, g, i: (g, 0, 0)),
        ],
        out_specs=pl.BlockSpec((tq, 2 * LANES), lambda b, g, i: (b * nq + i, g)),
        out_shape=jax.ShapeDtypeStruct((bsz * seq, A_WIDTH), BF16),
        scratch_shapes=[
            pltpu.VMEM((2, seq, 2 * LANES), BF16),
            pltpu.VMEM((seq // tq, VT_ROWS, tq), BF16),
            pltpu.VMEM((2, tq, LANES), BF16),
            pltpu.VMEM((tq, tq), F32),
            pltpu.VMEM((SUBLANES, tq), F32),
            pltpu.VMEM((A_HG, VT_ROWS, tq), F32),
            pltpu.VMEM((2, tq, 2 * LANES), BF16),
            pltpu.VMEM((A_HG, tq, tq), F32),
            pltpu.VMEM((A_HG, tq, tq), F32),
        ],
        compiler_params=_cparams(("parallel", "parallel", "arbitrary")),
        name="selected_attention",
    )(a0, sb, a0, gl, slopes)


def _win_kernel(q_ref, kv_ref, gl_ref, sl_ref, o_ref, ka_ref, vt_ref, bw_ref):
    tq = TQ
    nsub = q_ref.shape[0] // tq
    wk = WIN + tq
    npad = WIN // tq
    i = pl.program_id(2)

    @pl.when(i == 0)
    def _stage():
        lane_p = lax.broadcasted_iota(jnp.int32, (WIN, LANES), 1)
        for e in range(2):
            ka_ref[e, 0:WIN, 0:LANES] = jnp.zeros((WIN, LANES), BF16)
            ka_ref[e, 0:WIN, LANES:2 * LANES] = jnp.where(lane_p == 0, 1.0, 0.0).astype(BF16)
        for t in range(npad):
            vt_ref[t] = jnp.zeros(vt_ref.shape[1:], BF16)
        _stage_kv(kv_ref, ka_ref, vt_ref, WIN, tq, lambda t, e: jnp.zeros((tq, LANES), BF16))
        kk = lax.broadcasted_iota(jnp.int32, (wk, tq), 0)
        qq = lax.broadcasted_iota(jnp.int32, (wk, tq), 1)
        dist = qq + WIN - kk
        ok = (dist >= 0) & (dist < WIN)
        distf = dist.astype(F32)
        for hg in range(A_HG):
            bw_ref[hg] = jnp.where(ok, -sl_ref[hg:hg + 1, 0:1] * distf, NEG)

    lane = lax.broadcasted_iota(jnp.int32, (tq, LANES), 1)
    negrow = jnp.where(lane == 0, NEG, 0.0).astype(BF16)
    gate_t = jax.nn.sigmoid(gl_ref[...]).T
    units = [(sub, hg) for sub in range(nsub) for hg in range(A_HG)]
    scores = []
    for sub in range(nsub):
        k0 = pl.multiple_of((i * nsub + sub) * tq, tq)
        rows = slice(sub * tq, (sub + 1) * tq)
        qa = [jnp.concatenate([q_ref[rows, p * LANES:(p + 1) * LANES], negrow], axis=1) for p in range(2)]
        scores += [_dot_nt(ka_ref[hg % 2, pl.ds(k0, wk), :], qa[hg // 2]) for hg in range(A_HG)]
    probs = []
    for u, (sub, hg) in enumerate(units):
        s = scores[u] + bw_ref[hg]
        probs.append(jnp.exp2(s - jnp.max(s, axis=0, keepdims=True)).astype(BF16))
    outs = []
    for u, (sub, hg) in enumerate(units):
        t0 = i * nsub + sub
        o = _dot(vt_ref[t0], probs[u][0:tq, :])
        for j in range(1, wk // tq):
            o = o + _dot(vt_ref[t0 + j], probs[u][j * tq:(j + 1) * tq, :])
        gate = gate_t[2 * A_HG + hg:2 * A_HG + hg + 1, sub * tq:(sub + 1) * tq]
        outs.append(o[0:A_DH, :] * (gate / jnp.maximum(o[A_DH:A_DH + 1, :], 1e-30)))
    for sub in range(nsub):
        tile = jnp.concatenate(outs[sub * A_HG:(sub + 1) * A_HG], axis=0).T
        o_ref[sub * tq:(sub + 1) * tq, :] = tile.astype(o_ref.dtype)


def _win_attn(a0, gl, slopes, bsz, seq, kv_col):
    tb = WIN_SUBTILES * TQ
    nq = seq // tb
    return pl.pallas_call(
        _win_kernel,
        grid=(bsz, A_GROUPS, nq),
        in_specs=[
            pl.BlockSpec((tb, 2 * LANES), lambda b, g, i: (b * nq + i, g)),
            pl.BlockSpec((seq, LANES), lambda b, g, i: (b, kv_col + g)),
            pl.BlockSpec((tb, LANES), lambda b, g, i: (b * nq + i, g)),
            pl.BlockSpec((None, SUBLANES, LANES), lambda b, g, i: (g, 0, 0)),
        ],
        out_specs=pl.BlockSpec((tb, 2 * LANES), lambda b, g, i: (b * nq + i, g)),
        out_shape=jax.ShapeDtypeStruct((bsz * seq, A_WIDTH), BF16),
        scratch_shapes=[
            pltpu.VMEM((2, WIN + seq, 2 * LANES), BF16),
            pltpu.VMEM(((WIN + seq) // TQ, VT_ROWS, TQ), BF16),
            pltpu.VMEM((A_HG, WIN + TQ, TQ), F32),
        ],
        compiler_params=_cparams(("parallel", "parallel", "arbitrary")),
        name="window_attention",
    )(a0, a0, gl, slopes)


def _outproj0_kernel(oc_ref, os_ref, ow_ref, z_ref, x_ref, gate_ref, w_ref, o_ref):
    acc = None
    for c0 in range(0, oc_ref.shape[1], OUT_CHUNK):
        cols = slice(c0, c0 + OUT_CHUNK)
        o = oc_ref[:, cols].astype(F32) + os_ref[:, cols].astype(F32) + ow_ref[:, cols].astype(F32)
        y = (o * _silu(z_ref[:, cols].astype(F32))).astype(BF16)
        part = _dot(y, w_ref[cols, :])
        acc = part if acc is None else acc + part
    o_ref[...] = x_ref[...] + gate_ref[...] * acc


def _outproj0(oc, osl, ow, a0, x2d, gate, w, seq, z_col):
    m, d = x2d.shape
    tm = TM_OUT
    per_b = seq // tm
    row = lambda i: (i, 0)
    return pl.pallas_call(
        _outproj0_kernel,
        grid=(m // tm,),
        in_specs=[
            pl.BlockSpec((tm, A_WIDTH), row),
            pl.BlockSpec((tm, A_WIDTH), row),
            pl.BlockSpec((tm, A_WIDTH), row),
            pl.BlockSpec((tm, A_WIDTH), lambda i: (i, z_col)),
            pl.BlockSpec((tm, d), row),
            pl.BlockSpec((None, 1, d), lambda i: (i // per_b, 0, 0)),
            pl.BlockSpec(w.shape, lambda i: (0, 0)),
        ],
        out_specs=pl.BlockSpec((tm, d), row),
        out_shape=jax.ShapeDtypeStruct((m, d), F32),
        compiler_params=_cparams(("parallel",)),
        name="nsa_out_proj",
    )(oc, osl, ow, a0, x2d, gate[:, None, :], w)


def _conv_kernel(x_ref, halo_ref, w_ref, b_ref, ks_ref, o_ref, *, per_b):
    first = (pl.program_id(0) % per_b) == 0
    halo = jnp.where(first, 0.0, halo_ref[...].astype(F32))
    xe = jnp.concatenate([halo, x_ref[...].astype(F32)], axis=0)
    y = xe[SUBLANES:, :] * w_ref[CONV_W - 1:CONV_W, :] + b_ref[...]
    for s in range(1, CONV_W):
        y = y + pltpu.roll(xe, s, 0)[SUBLANES:, :] * w_ref[CONV_W - 1 - s:CONV_W - s, :]
    o_ref[...] = (_silu(y) * ks_ref[...]).astype(o_ref.dtype)


def _conv_silu(a1, conv_w, conv_b, kscale, seq):
    m = a1.shape[0]
    tm, tc = TM_CONV, TC_CONV
    per_b = seq // tm
    hb = tm // SUBLANES
    kern = functools.partial(_conv_kernel, per_b=per_b)
    return pl.pallas_call(
        kern,
        grid=(m // tm, B_QK // tc),
        in_specs=[
            pl.BlockSpec((tm, tc), lambda i, j: (i, j)),
            pl.BlockSpec((SUBLANES, tc), lambda i, j: (jnp.maximum(i * hb - 1, 0), j)),
            pl.BlockSpec((CONV_W, tc), lambda i, j: (0, j)),
            pl.BlockSpec((1, tc), lambda i, j: (0, j)),
            pl.BlockSpec((1, tc), lambda i, j: (0, j)),
        ],
        out_specs=pl.BlockSpec((tm, tc), lambda i, j: (i, j)),
        out_shape=jax.ShapeDtypeStruct((m, B_QK), BF16),
        compiler_params=_cparams(("parallel", "parallel")),
        name="causal_conv_silu",
    )(a1, a1, conv_w, conv_b.reshape(1, B_QK), kscale)


def _mlstm_kernel(q_ref, k_ref, v_ref, g_ref, gb_ref, hg_ref, o_ref, c_ref, n_ref, m_ref):
    ln = q_ref.shape[0]
    nh = B_HEADS

    @pl.when(pl.program_id(1) == 0)
    def _():
        c_ref[...] = jnp.zeros(c_ref.shape, F32)
        n_ref[...] = jnp.zeros(n_ref.shape, F32)
        m_ref[...] = jnp.zeros(m_ref.shape, F32)

    lane = lax.broadcasted_iota(jnp.int32, (ln, LANES), 1)
    gi = (g_ref[:, 0:LANES] + gb_ref[:, 0:LANES]) * LOG2E
    gf = g_ref[:, LANES:2 * LANES] + gb_ref[:, LANES:2 * LANES]
    lf = (jnp.minimum(gf, 0.0) - jnp.log1p(jnp.exp(-jnp.abs(gf)))) * LOG2E
    r = lax.broadcasted_iota(jnp.int32, (ln, ln), 0)
    c = lax.broadcasted_iota(jnp.int32, (ln, ln), 1)
    causal = c <= r
    bcum = _dot(jnp.where(causal, 1.0, 0.0), lf, precision=HIGHEST)
    rel_t = (gi - bcum).T
    pm = rel_t[0:SUBLANES, :]
    lane_r = lax.broadcasted_iota(jnp.int32, pm.shape, 1)
    step = 1
    while step < ln:
        pm = jnp.maximum(pm, jnp.where(lane_r >= step, pltpu.roll(pm, step, 1), NEG))
        step *= 2
    pmax = jnp.concatenate([pm, jnp.zeros((LANES - SUBLANES, ln), F32)], axis=0).T
    m_st = m_ref[...]
    g_tot = bcum[ln - 1:ln, :]
    m_inter = bcum + m_st
    m_j = jnp.maximum(bcum + pmax, m_inter)
    e_arg = bcum - m_j
    w_inter = jnp.exp2(m_inter - m_j)
    a_all = g_tot - bcum + gi
    m_loc = jnp.max(a_all, axis=0, keepdims=True)
    w_all = jnp.exp2(a_all - m_loc)
    m_new = jnp.maximum(g_tot + m_st, m_loc)
    s_old = jnp.exp2(g_tot + m_st - m_new)
    s_new = jnp.exp2(m_loc - m_new)
    m_ref[...] = m_new

    def head(ref, h, width):
        return ref[:, h * width:(h + 1) * width]

    def onehot_col(h):
        return jnp.where(lane == h, 1.0, 0.0).astype(BF16)

    sqk = [_dot_nt(head(q_ref, h, B_DK), head(k_ref, h, B_DK)) for h in range(nh)]
    qc = [_dot(head(q_ref, h, B_DK), c_ref[h].astype(BF16)) for h in range(nh)]
    qn = _dot(head(q_ref, 0, B_DK), n_ref[0].astype(BF16))
    for h in range(1, nh):
        qn = qn + _dot(head(q_ref, h, B_DK), n_ref[h].astype(BF16))

    smats, kwts = [], []
    rowsum = jnp.zeros((ln, LANES), F32)
    for h in range(nh):
        log_w = jnp.where(causal, e_arg[:, h:h + 1] + rel_t[h:h + 1, :], NEG)
        smat = sqk[h] * jnp.exp2(log_w)
        rowsum = jnp.where(lane == h, jnp.sum(smat, axis=1, keepdims=True), rowsum)
        smats.append(smat.astype(BF16))
        kw = head(k_ref, h, B_DK).astype(F32) * w_all[:, h:h + 1]
        kwts.append(kw.T.astype(BF16))

    intra = [_dot(smats[h], head(v_ref, h, B_DV)) for h in range(nh)]
    c_loc = [_dot(kwts[h], head(v_ref, h, B_DV)) for h in range(nh)]
    n_loc = [_dot(kwts[h], onehot_col(h)) for h in range(nh)]

    us = []
    msq = jnp.zeros((ln, LANES), F32)
    for h in range(nh):
        c_ref[h] = s_old[:, h:h + 1] * c_ref[h] + s_new[:, h:h + 1] * c_loc[h]
        n_ref[h] = s_old[:, h:h + 1] * n_ref[h] + s_new[:, h:h + 1] * n_loc[h]
        u = intra[h] + w_inter[:, h:h + 1] * qc[h]
        msq = jnp.where(lane == h, jnp.mean(u * u, axis=-1, keepdims=True), msq)
        us.append(u)
    den = rowsum + w_inter * qn
    scale = 1.0 / jnp.maximum(jnp.abs(den), jnp.exp2(-m_j))
    f = scale * lax.rsqrt(scale * scale * msq + EPS)
    for h in range(nh):
        o_ref[:, h * B_DV:(h + 1) * B_DV] = (us[h] * f[:, h:h + 1] * head(hg_ref, h, B_DV)).astype(o_ref.dtype)


def _mlstm(qk, a1, g1, gbias, head_g, bsz, seq):
    ln = ML_CHUNK
    nc = seq // ln
    wq = B_QK // 2
    return pl.pallas_call(
        _mlstm_kernel,
        grid=(bsz, nc),
        in_specs=[
            pl.BlockSpec((ln, wq), lambda b, n: (b * nc + n, 0)),
            pl.BlockSpec((ln, wq), lambda b, n: (b * nc + n, 1)),
            pl.BlockSpec((ln, B_WIDTH), lambda b, n: (b * nc + n, B_QK // B_WIDTH)),
            pl.BlockSpec((ln, 2 * LANES), lambda b, n: (b * nc + n, 0)),
            pl.BlockSpec((1, 2 * LANES), lambda b, n: (0, 0)),
            pl.BlockSpec((1, B_WIDTH), lambda b, n: (0, 0)),
        ],
        out_specs=pl.BlockSpec((ln, B_WIDTH), lambda b, n: (b * nc + n, 0)),
        out_shape=jax.ShapeDtypeStruct((bsz * seq, B_WIDTH), BF16),
        scratch_shapes=[
            pltpu.VMEM((B_HEADS, B_DK, B_DV), F32),
            pltpu.VMEM((B_HEADS, B_DK, LANES), F32),
            pltpu.VMEM((1, LANES), F32),
        ],
        compiler_params=_cparams(("parallel", "arbitrary")),
        name="mlstm_chunkwise",
    )(qk, qk, a1, g1, gbias, head_g.reshape(1, B_WIDTH))


def _outproj1_kernel(h_ref, og_ref, z_ref, x_ref, gate_ref, w_ref, fg_ref, o_ref):
    acc = None
    for c0 in range(0, h_ref.shape[1], OUT_CHUNK):
        cols = slice(c0, c0 + OUT_CHUNK)
        y = (jax.nn.sigmoid(og_ref[:, cols].astype(F32)) * h_ref[:, cols].astype(F32)
             * _silu(z_ref[:, cols].astype(F32)))
        part = _dot(y.astype(BF16), w_ref[cols, :])
        acc = part if acc is None else acc + part
    x2 = x_ref[...] + gate_ref[...] * acc
    o_ref[...] = x2 * lax.rsqrt(jnp.mean(x2 * x2, axis=-1, keepdims=True) + EPS) * fg_ref[...]


def _outproj1(hn, a1, x2d, gate, w, final_g, seq):
    m, d = x2d.shape
    tm = TM_OUT
    per_b = seq // tm
    row = lambda i: (i, 0)
    og_col = B_QK // B_WIDTH + 1
    return pl.pallas_call(
        _outproj1_kernel,
        grid=(m // tm,),
        in_specs=[
            pl.BlockSpec((tm, B_WIDTH), row),
            pl.BlockSpec((tm, B_WIDTH), lambda i: (i, og_col)),
            pl.BlockSpec((tm, B_WIDTH), lambda i: (i, og_col + 1)),
            pl.BlockSpec((tm, d), row),
            pl.BlockSpec((None, 1, d), lambda i: (i // per_b, 0, 0)),
            pl.BlockSpec(w.shape, lambda i: (0, 0)),
            pl.BlockSpec((1, d), lambda i: (0, 0)),
        ],
        out_specs=pl.BlockSpec((tm, d), row),
        out_shape=jax.ShapeDtypeStruct((m, d), F32),
        compiler_params=_cparams(("parallel",)),
        name="mlstm_out_proj_final_norm",
    )(hn, a1, a1, x2d, gate[:, None, :], w, final_g.reshape(1, d))


def _alibi_slopes():
    return np.asarray(2.0 ** (-8.0 * np.arange(1, A_HEADS + 1) / A_HEADS), np.float32)


def _selection_overlap_t(seq, cols):
    n_cmp = seq // CMP_STRIDE - CMP_LEN // CMP_STRIDE + 1
    n_sel = seq // SLC_LEN
    c0 = np.arange(n_cmp)[None, :] * CMP_STRIDE
    s0 = np.arange(n_sel)[:, None] * SLC_LEN
    ov = np.clip(np.minimum(c0 + CMP_LEN, s0 + SLC_LEN) - np.maximum(c0, s0), 0, None) / CMP_LEN
    out = np.zeros((n_sel, cols), np.float32)
    out[:, :n_cmp] = ov
    return out


def _layer0_weights(w_in):
    d = w_in.shape[0]
    kv0 = A_WIDTH
    gl0 = kv0 + 6 * A_GROUPS * A_DH
    z0 = gl0 + 3 * A_HEADS
    wq = w_in[:, :A_WIDTH] * (A_DH ** -0.5 * LOG2E)
    wz = w_in[:, z0:z0 + A_WIDTH]
    wkv = w_in[:, kv0:gl0].reshape(d, 6, A_GROUPS, A_DH)
    branches = [wkv[:, 2 * r:2 * r + 2].transpose(0, 2, 1, 3).reshape(d, A_GROUPS * 2 * A_DH) for r in range(3)]
    w0 = jnp.concatenate([wq, wz] + branches, axis=1).astype(BF16)
    wgl = w_in[:, gl0:z0].reshape(d, 3, A_GROUPS, A_HG).transpose(0, 2, 1, 3).reshape(d, A_GROUPS, 3 * A_HG)
    wgl = jnp.pad(wgl, ((0, 0), (0, 0), (0, LANES - 3 * A_HG))).reshape(d, A_GROUPS * LANES).astype(BF16)
    return w0, wgl


def _compress_weights(pe, w1, b1, w2, b2):
    half = CMP_LEN // 2
    w1r = w1.reshape(2, 2, half, A_DH, CMP_HIDDEN)
    wc = jnp.einsum("khldn,kq->lkdqhn", w1r, jnp.eye(2, dtype=w1.dtype))
    wc = wc.reshape(half * 2 * A_DH, 4 * CMP_HIDDEN).astype(BF16)
    pec = pe.reshape(2, 2, half, A_DH).transpose(1, 2, 0, 3).reshape(2, half * 2 * A_DH)
    pec = jnp.pad(pec, ((0, SUBLANES - 2), (0, 0))).astype(BF16)
    b1c = b1.reshape(1, 2 * CMP_HIDDEN)
    w2k = jnp.zeros((CMP_HIDDEN, 2 * LANES), F32)
    w2k = w2k.at[:, 0:A_DH].set(w2[0]).at[:, LANES + A_DH:2 * LANES].set(w2[0])
    b2k = jnp.zeros((1, 2 * LANES), F32)
    b2k = b2k.at[0, 0:A_DH].set(b2[0]).at[0, LANES + A_DH:2 * LANES].set(b2[0])
    w2vt = w2[1].T.astype(BF16)
    b2v = jnp.broadcast_to(b2[1][:, None], (A_DH, LANES))
    return wc, pec, b1c, w2k.astype(BF16), b2k, w2vt, b2v


def _layer1_weights(w_in):
    g0 = B_QK + B_WIDTH
    w_main = jnp.concatenate([w_in[:, :g0], w_in[:, g0 + 2 * B_HEADS:]], axis=1).astype(BF16)
    pad = ((0, 0), (0, LANES - B_HEADS))
    wg = jnp.concatenate([jnp.pad(w_in[:, g0:g0 + B_HEADS], pad),
                          jnp.pad(w_in[:, g0 + B_HEADS:g0 + 2 * B_HEADS], pad)], axis=1).astype(BF16)
    return w_main, wg


def kernel(x, c, ada_w, ada_b, norm_g, final_g, a_w_in, a_cmp_pe, a_cmp_w1, a_cmp_b1, a_cmp_w2, a_cmp_b2,
           a_w_out, b_w_in, b_conv_w, b_conv_b, b_gate_b, b_head_g, b_w_out):
    bsz, seq, d = x.shape
    assert d == D_MODEL and seq % (2 * TM_PROJ) == 0 and ada_w.shape[0] == 2
    m = bsz * seq
    x2d = x.reshape(m, d)
    mod = _ada_mod(c, ada_w, ada_b)

    w0, wgl = _layer0_weights(a_w_in[0])
    a0, gl = _normmod_matmul(x2d, norm_g[0], mod[0, :, d:2 * d], mod[0, :, :d], w0, wgl, seq, TN_PROJ0, "nsa_in_proj")
    z_col = 1
    kv_col = 2 * A_WIDTH // LANES
    n_chunk = seq // CMP_STRIDE
    n_cmp = n_chunk - CMP_LEN // CMP_STRIDE + 1
    n_sel = seq // SLC_LEN
    n_top = min(SLC_TOP, n_sel)
    assert n_sel % SUBLANES == 0 and n_sel <= LANES and n_top >= 3
    cmp_col = 2 * A_WIDTH // (A_GROUPS * LANES)
    kc, vct = _compress(a0, cmp_col, bsz, seq,
                        *_compress_weights(a_cmp_pe[0], a_cmp_w1[0], a_cmp_b1[0], a_cmp_w2[0], a_cmp_b2[0]))
    slopes = jnp.asarray(np.broadcast_to(
        np.pad((_alibi_slopes() * np.float32(LOG2E)).reshape(A_GROUPS, A_HG), ((0, 0), (0, SUBLANES - A_HG)))[:, :, None],
        (A_GROUPS, SUBLANES, LANES)))
    ovt = jnp.asarray(_selection_overlap_t(seq, n_chunk))
    o_cmp, sb = _cmp_attn(a0, kc, vct, gl, slopes, ovt, bsz, seq, n_cmp, n_sel, n_top)
    o_slc = _slc_attn(a0, sb, gl, slopes, bsz, seq, kv_col + A_GROUPS)
    o_win = _win_attn(a0, gl, slopes, bsz, seq, kv_col + 2 * A_GROUPS)
    x1 = _outproj0(o_cmp, o_slc, o_win, a0, x2d, mod[0, :, 2 * d:], a_w_out[0].astype(BF16), seq, z_col)

    w1m, w1g = _layer1_weights(b_w_in[0])
    a1, g1 = _normmod_matmul(x1, norm_g[1], mod[1, :, d:2 * d], mod[1, :, :d], w1m, w1g, seq, TN_PROJ1, "mlstm_in_proj")
    kscale = jnp.concatenate([jnp.ones((1, B_QK // 2), F32), jnp.full((1, B_QK // 2), B_DK ** -0.5, F32)], axis=1)
    qk = _conv_silu(a1, b_conv_w[0], b_conv_b[0], kscale, seq)
    gbias = jnp.pad(b_gate_b[0], ((0, 0), (0, LANES - B_HEADS))).reshape(1, 2 * LANES)
    y1 = _mlstm(qk, a1, g1, gbias, b_head_g[0], bsz, seq)
    out = _outproj1(y1, a1, x1, mod[1, :, 2 * d:], b_w_out[0].astype(BF16), final_g, seq)
    return out.reshape(bsz, seq, d)
```
